```python
import math, functools
import jax, jax.numpy as jnp
from jax import lax
import numpy as np

D_MODEL = 1024
BATCH = 1
SEQ = 16384
DEPTH = 2
DEC_BATCH = 8
DEC_SEQ = 64
PAST_LEN = 1024

CHUNK = 64
CONV_CH = 512
CONV_WIDTH = 31
DN_HEADS = 4
DN_DK = 128
DN_DV = 128
DN_CONV = 4
DN_QKV = DN_HEADS * (2 * DN_DK + DN_DV)
ATT_HEADS = 8
ATT_HD = 64
ATT_W = ATT_HEADS * ATT_HD
ATT_PREV = 8
BAND = (ATT_PREV + 1) * CHUNK
REL_MAX = 128
REL_MIN = -(CHUNK - 1)
N_REL = REL_MAX - REL_MIN + 1
D_FF = 2816
N_EXPERTS = 8
TOP_K = 2
N_DENSE = (DEPTH + 1) // 2
N_MOE = DEPTH // 2
N_BRANCH = 3
EPS = 1e-6
NEG_INF = -1e30
OFF_A = 0
OFF_BQKV = OFF_A + 2 * CONV_CH
OFF_BA = OFF_BQKV + DN_QKV
OFF_BB = OFF_BA + DN_HEADS
OFF_BG = OFF_BB + DN_HEADS
OFF_C = OFF_BG + DN_HEADS * DN_DV
OFF_GATE = OFF_C + 3 * ATT_W
IN_COLS = OFF_GATE + N_BRANCH * D_MODEL

kernel_name = 'hybrid_streaming_conv_delta_chunkattn_step'


def rmsnorm(x, g):
    xf = x.astype(jnp.float32)
    y = xf * lax.rsqrt(jnp.mean(xf * xf, axis=-1, keepdims=True) + EPS)
    return y.astype(x.dtype) * g


def layernorm(x, g, b):
    xf = x.astype(jnp.float32)
    xc = xf - jnp.mean(xf, axis=-1, keepdims=True)
    y = xc * lax.rsqrt(jnp.mean(xc * xc, axis=-1, keepdims=True) + EPS)
    return y.astype(x.dtype) * g + b


def l2norm(x):
    xf = x.astype(jnp.float32)
    return xf * lax.rsqrt(jnp.sum(xf * xf, axis=-1, keepdims=True) + EPS)


def causal_dwconv(x, buf, w):
    k_w = w.shape[0]
    xp = jnp.concatenate([buf.astype(x.dtype), x], axis=1)
    y = lax.conv_general_dilated(xp, w[:, None, :].astype(x.dtype), window_strides=(1,), padding='VALID',
                                 dimension_numbers=('NWC', 'WIO', 'NWC'), feature_group_count=x.shape[-1])
    return y, xp[:, xp.shape[1] - (k_w - 1):]


def gated_delta_rule(q, k, v, g, beta, s0):
    b, t, h, dk = q.shape
    dv = v.shape[-1]
    c = min(CHUNK, t)
    n = t // c
    f32 = jnp.float32
    q = q.astype(f32).reshape(b, n, c, h, dk)
    k = k.astype(f32).reshape(b, n, c, h, dk)
    v = v.astype(f32).reshape(b, n, c, h, dv)
    beta = beta.astype(f32).reshape(b, n, c, h)
    gc = jnp.cumsum(g.astype(f32).reshape(b, n, c, h), axis=2)
    gh = jnp.swapaxes(gc, 2, 3)
    tril = jnp.tril(jnp.ones((c, c), bool))
    strict = jnp.tril(jnp.ones((c, c), bool), -1)
    gam = jnp.exp(jnp.where(tril, gh[..., :, None] - gh[..., None, :], -jnp.inf))
    kb = k * beta[..., None]
    a_low = jnp.where(strict, jnp.einsum('bnihd,bnjhd->bnhij', kb, k) * gam, 0.0)
    solve = functools.partial(lax.linalg.triangular_solve, left_side=True, lower=True, unit_diagonal=True)
    u = solve(a_low, jnp.swapaxes(v * beta[..., None], 2, 3))
    w = solve(a_low, jnp.swapaxes(kb * jnp.exp(gc)[..., None], 2, 3))
    qk = jnp.einsum('bnihd,bnjhd->bnhij', q, k) * gam
    qg = jnp.swapaxes(q * jnp.exp(gc)[..., None], 2, 3)
    kd = jnp.swapaxes(k * jnp.exp(gc[:, :, -1:] - gc)[..., None], 2, 3)
    dl = jnp.exp(gc[:, :, -1])

    def step(s, xs):
        u_i, w_i, qk_i, qg_i, kd_i, dl_i = xs
        vn = u_i - jnp.einsum('bhck,bhkv->bhcv', w_i, s)
        o = jnp.einsum('bhck,bhkv->bhcv', qg_i, s) + jnp.einsum('bhij,bhjv->bhiv', qk_i, vn)
        s = s * dl_i[:, :, None, None] + jnp.einsum('bhck,bhcv->bhkv', kd_i, vn)
        return s, o

    xs = tuple(jnp.moveaxis(a, 1, 0) for a in (u, w, qk, qg, kd, dl))
    s_fin, o = lax.scan(step, s0.astype(f32), xs)
    o = jnp.transpose(o, (1, 0, 3, 2, 4)).reshape(b, t, h, dv)
    return o, s_fin


def rel_bias_matrix(rel_bias, n_q, key_offset, n_k):
    a = jnp.arange(n_q)[:, None]
    j = jnp.arange(n_k)[None, :]
    idx = jnp.clip(key_offset + a - j, REL_MIN, REL_MAX) - REL_MIN
    return rel_bias[:, idx].astype(jnp.float32)


def band_chunk_attention(q, k, v, rel_bias):
    b, t, h, d = q.shape
    nc = t // CHUNK
    qc = q.reshape(b, nc, CHUNK, h, d)
    pad = ((0, 0), (ATT_PREV, 0), (0, 0), (0, 0), (0, 0))
    kp = jnp.pad(k.reshape(b, nc, CHUNK, h, d), pad)
    vp = jnp.pad(v.reshape(b, nc, CHUNK, h, d), pad)
    kband = jnp.concatenate([kp[:, i:i + nc] for i in range(ATT_PREV + 1)], axis=2)
    vband = jnp.concatenate([vp[:, i:i + nc] for i in range(ATT_PREV + 1)], axis=2)
    s = jnp.einsum('bnqhd,bnkhd->bnhqk', qc, kband, preferred_element_type=jnp.float32) * (ATT_HD ** -0.5)
    bias = rel_bias_matrix(rel_bias, CHUNK, ATT_PREV * CHUNK, BAND)
    key_chunk = jnp.arange(nc)[:, None] - ATT_PREV + jnp.arange(BAND)[None, :] // CHUNK
    valid = key_chunk >= 0
    s = jnp.where(valid[None, :, None, None, :], s + bias[None, None], NEG_INF)
    p = jax.nn.softmax(s, axis=-1).astype(v.dtype)
    o = jnp.einsum('bnhqk,bnkhd->bnqhd', p, vband)
    return o.reshape(b, t, h, d)


def cached_chunk_attention(q, k, v, ck, cv, rel_bias):
    wc = ck.shape[1]
    l = q.shape[1]
    kk = jnp.concatenate([ck.astype(k.dtype), k], axis=1)
    vv = jnp.concatenate([cv.astype(v.dtype), v], axis=1)
    s = jnp.einsum('bqhd,bkhd->bhqk', q, kk, preferred_element_type=jnp.float32) * (ATT_HD ** -0.5)
    s = s + rel_bias_matrix(rel_bias, l, wc, wc + l)[None]
    p = jax.nn.softmax(s, axis=-1).astype(vv.dtype)
    return jnp.einsum('bhqk,bkhd->bqhd', p, vv)


def token_mixers(xn, p, conv_buf, sconv_buf, s0, kv_cache):
    b, t, _ = xn.shape
    h = xn @ p['w_in']
    a_in = h[..., OFF_A:OFF_A + CONV_CH] * jax.nn.sigmoid(h[..., OFF_A + CONV_CH:OFF_A + 2 * CONV_CH])
    a_conv, conv_new = causal_dwconv(a_in, conv_buf, p['dw_a'])
    a_act = jax.nn.silu(layernorm(a_conv + p['dwb_a'], p['ln_a_g'], p['ln_a_b']))
    y_a = a_act @ p['w_a_out']
    qkv, sconv_new = causal_dwconv(h[..., OFF_BQKV:OFF_BQKV + DN_QKV], sconv_buf, p['dw_b'])
    qkv = jax.nn.silu(qkv)
    qb = l2norm(qkv[..., :DN_HEADS * DN_DK].reshape(b, t, DN_HEADS, DN_DK)) * (DN_DK ** -0.5)
    kb = l2norm(qkv[..., DN_HEADS * DN_DK:2 * DN_HEADS * DN_DK].reshape(b, t, DN_HEADS, DN_DK))
    vb = qkv[..., 2 * DN_HEADS * DN_DK:].reshape(b, t, DN_HEADS, DN_DV)
    alpha_raw = h[..., OFF_BA:OFF_BA + DN_HEADS].astype(jnp.float32)
    beta = jax.nn.sigmoid(h[..., OFF_BB:OFF_BB + DN_HEADS].astype(jnp.float32))
    g = -jnp.exp(p['a_log'].astype(jnp.float32)) * jax.nn.softplus(alpha_raw + p['dt_bias'].astype(jnp.float32))
    ob, s_new = gated_delta_rule(qb, kb, vb, g, beta, s0)
    gate_b = h[..., OFF_BG:OFF_BG + DN_HEADS * DN_DV].reshape(b, t, DN_HEADS, DN_DV)
    ob = rmsnorm(ob.astype(xn.dtype), p['onorm_b']) * jax.nn.silu(gate_b)
    y_b = ob.reshape(b, t, DN_HEADS * DN_DV) @ p['w_b_out']
    qc = h[..., OFF_C:OFF_C + ATT_W].reshape(b, t, ATT_HEADS, ATT_HD)
    kc = h[..., OFF_C + ATT_W:OFF_C + 2 * ATT_W].reshape(b, t, ATT_HEADS, ATT_HD)
    vc = h[..., OFF_C + 2 * ATT_W:OFF_C + 3 * ATT_W].reshape(b, t, ATT_HEADS, ATT_HD)
    if kv_cache is None:
        oc = band_chunk_attention(qc, kc, vc, p['rel_bias'])
        keep = min(ATT_PREV * CHUNK, t)
        k_new, v_new = kc[:, t - keep:], vc[:, t - keep:]
    else:
        oc = cached_chunk_attention(qc, kc, vc, kv_cache[0], kv_cache[1], p['rel_bias'])
        k_new, v_new = kc, vc
    y_c = oc.reshape(b, t, ATT_W) @ p['w_c_out']
    gates = jax.nn.sigmoid(h[..., OFF_GATE:].reshape(b, t, N_BRANCH, D_MODEL) + p['b_gate'])
    merged = gates[..., 0, :] * y_a + gates[..., 1, :] * y_b + gates[..., 2, :] * y_c
    return merged @ p['w_out'], (conv_new, sconv_new, s_new.astype(xn.dtype), k_new, v_new)


def swiglu(x, w1, w3, w2):
    return (jax.nn.silu(x @ w1) * (x @ w3)) @ w2


def moe_swiglu(x, router, w1, w3, w2):
    logits = (x @ router).astype(jnp.float32)
    top_v, top_i = lax.top_k(logits, TOP_K)
    wts = jax.nn.softmax(top_v, axis=-1)
    y = jnp.zeros_like(x)
    for e in range(N_EXPERTS):
        ge = jnp.sum(jnp.where(top_i == e, wts, 0.0), axis=-1).astype(x.dtype)
        y = y + ge[..., None] * swiglu(x, w1[e], w3[e], w2[e])
    return y


def setup_inputs(seed: int = 0) -> dict:
    key = jax.random.key(seed)
    ks = iter(jax.random.split(key, 40))

    def nrm(shape, scale):
        return scale * jax.random.normal(next(ks), shape, jnp.float32)

    w_cache = min(ATT_PREV * CHUNK, PAST_LEN)
    dt = jnp.exp(jax.random.uniform(next(ks), (DEPTH, DN_HEADS)) * (math.log(0.1) - math.log(1e-3)) + math.log(1e-3))
    return {
        'x_prompt': nrm((BATCH, SEQ, D_MODEL), 1.0),
        'x_sample': nrm((DEC_BATCH, DEC_SEQ, D_MODEL), 1.0),
        'cache_conv_a': nrm((DEPTH, DEC_BATCH, CONV_WIDTH - 1, CONV_CH), 0.5),
        'state_sconv_b': nrm((DEPTH, DEC_BATCH, DN_CONV - 1, DN_QKV), 1.0),
        'state_delta_b': nrm((DEPTH, DEC_BATCH, DN_HEADS, DN_DK, DN_DV), 0.1),
        'cache_k_c': nrm((DEPTH, DEC_BATCH, w_cache, ATT_HEADS, ATT_HD), 1.0),
        'cache_v_c': nrm((DEPTH, DEC_BATCH, w_cache, ATT_HEADS, ATT_HD), 1.0),
        'norm1_g': 1.0 + nrm((DEPTH, D_MODEL), 0.02),
        'w_in': nrm((DEPTH, D_MODEL, IN_COLS), D_MODEL ** -0.5),
        'b_gate': nrm((DEPTH, N_BRANCH, D_MODEL), 0.1),
        'dw_a': nrm((DEPTH, CONV_WIDTH, CONV_CH), CONV_WIDTH ** -0.5),
        'dwb_a': nrm((DEPTH, CONV_CH), 0.02),
        'ln_a_g': 1.0 + nrm((DEPTH, CONV_CH), 0.02),
        'ln_a_b': nrm((DEPTH, CONV_CH), 0.02),
        'w_a_out': nrm((DEPTH, CONV_CH, D_MODEL), CONV_CH ** -0.5),
        'dw_b': nrm((DEPTH, DN_CONV, DN_QKV), DN_CONV ** -0.5),
        'a_log': jnp.log(jax.random.uniform(next(ks), (DEPTH, DN_HEADS), minval=1.0, maxval=16.0)),
        'dt_bias': dt + jnp.log(-jnp.expm1(-dt)),
        'onorm_b': 1.0 + nrm((DEPTH, DN_DV), 0.02),
        'w_b_out': nrm((DEPTH, DN_HEADS * DN_DV, D_MODEL), (DN_HEADS * DN_DV) ** -0.5),
        'rel_bias': nrm((DEPTH, ATT_HEADS, N_REL), 0.5),
        'w_c_out': nrm((DEPTH, ATT_W, D_MODEL), ATT_W ** -0.5),
        'w_out': nrm((DEPTH, D_MODEL, D_MODEL), D_MODEL ** -0.5),
        'norm2_g': 1.0 + nrm((DEPTH, D_MODEL), 0.02),
        'ffn_w1': nrm((N_DENSE, D_MODEL, D_FF), D_MODEL ** -0.5),
        'ffn_w3': nrm((N_DENSE, D_MODEL, D_FF), D_MODEL ** -0.5),
        'ffn_w2': nrm((N_DENSE, D_FF, D_MODEL), D_FF ** -0.5),
        'router': nrm((N_MOE, D_MODEL, N_EXPERTS), D_MODEL ** -0.5),
        'moe_w1': nrm((N_MOE, N_EXPERTS, D_MODEL, D_FF), D_MODEL ** -0.5),
        'moe_w3': nrm((N_MOE, N_EXPERTS, D_MODEL, D_FF), D_MODEL ** -0.5),
        'moe_w2': nrm((N_MOE, N_EXPERTS, D_FF, D_MODEL), D_FF ** -0.5),
        'final_norm_g': 1.0 + nrm((D_MODEL,), 0.02),
    }


def reference(x_prompt, x_sample, cache_conv_a, state_sconv_b, state_delta_b, cache_k_c, cache_v_c,
              norm1_g, w_in, b_gate, dw_a, dwb_a, ln_a_g, ln_a_b, w_a_out, dw_b, a_log, dt_bias, onorm_b,
              w_b_out, rel_bias, w_c_out, w_out, norm2_g, ffn_w1, ffn_w3, ffn_w2, router, moe_w1, moe_w3,
              moe_w2, final_norm_g):
    xp, xs = x_prompt, x_sample
    bp = xp.shape[0]
    p_st = [[], [], [], [], []]
    s_st = [[], [], [], [], []]
    for l in range(DEPTH):
        p = {'w_in': w_in[l], 'b_gate': b_gate[l], 'dw_a': dw_a[l], 'dwb_a': dwb_a[l], 'ln_a_g': ln_a_g[l],
             'ln_a_b': ln_a_b[l], 'w_a_out': w_a_out[l], 'dw_b': dw_b[l], 'a_log': a_log[l],
             'dt_bias': dt_bias[l], 'onorm_b': onorm_b[l], 'w_b_out': w_b_out[l], 'rel_bias': rel_bias[l],
             'w_c_out': w_c_out[l], 'w_out': w_out[l]}
        z_conv = jnp.zeros((bp, CONV_WIDTH - 1, CONV_CH), xp.dtype)
        z_sconv = jnp.zeros((bp, DN_CONV - 1, DN_QKV), xp.dtype)
        z_s = jnp.zeros((bp, DN_HEADS, DN_DK, DN_DV), xp.dtype)
        mix_p, st_p = token_mixers(rmsnorm(xp, norm1_g[l]), p, z_conv, z_sconv, z_s, None)
        xp = xp + mix_p
        mix_s, st_s = token_mixers(rmsnorm(xs, norm1_g[l]), p, cache_conv_a[l], state_sconv_b[l],
                                   state_delta_b[l], (cache_k_c[l], cache_v_c[l]))
        xs = xs + mix_s
        for i in range(5):
            p_st[i].append(st_p[i])
            s_st[i].append(st_s[i])
        hp = rmsnorm(xp, norm2_g[l])
        hs = rmsnorm(xs, norm2_g[l])
        j = l // 2
        if l % 2 == 0:
            xp = xp + swiglu(hp, ffn_w1[j], ffn_w3[j], ffn_w2[j])
            xs = xs + swiglu(hs, ffn_w1[j], ffn_w3[j], ffn_w2[j])
        else:
            xp = xp + moe_swiglu(hp, router[j], moe_w1[j], moe_w3[j], moe_w2[j])
            xs = xs + moe_swiglu(hs, router[j], moe_w1[j], moe_w3[j], moe_w2[j])
    y_prompt = rmsnorm(xp, final_norm_g)
    y_sample = rmsnorm(xs, final_norm_g)
    p_conv_a = jnp.stack(p_st[0])
    p_sconv_b = jnp.stack(p_st[1])
    p_delta_b = jnp.stack(p_st[2])
    p_k_c = jnp.stack(p_st[3])
    p_v_c = jnp.stack(p_st[4])
    s_conv_a = jnp.stack(s_st[0])
    s_sconv_b = jnp.stack(s_st[1])
    s_delta_b = jnp.stack(s_st[2])
    s_k_c = jnp.stack(s_st[3])
    s_v_c = jnp.stack(s_st[4])
    return (y_prompt, y_sample, p_conv_a, p_sconv_b, p_delta_b, p_k_c, p_v_c, s_conv_a, s_sconv_b, s_delta_b, s_k_c, s_v_c)
```

```python
import functools

import jax
import jax.numpy as jnp
from jax import lax
from jax.experimental import pallas as pl
from jax.experimental.pallas import tpu as pltpu

F32 = jnp.float32
BF16 = jnp.bfloat16

D_MODEL = 1024
CHUNK = 64
CONV_CH = 512
CONV_WIDTH = 31
DN_HEADS = 4
DN_DK = 128
DN_DV = 128
DN_CONV = 4
DN_QK = DN_HEADS * DN_DK
DN_QKV = DN_HEADS * (2 * DN_DK + DN_DV)
ATT_HEADS = 8
ATT_HD = 64
ATT_W = ATT_HEADS * ATT_HD
ATT_PREV = 8
BAND = (ATT_PREV + 1) * CHUNK
REL_MAX = 128
REL_MIN = -(CHUNK - 1)
N_REL = REL_MAX - REL_MIN + 1
D_FF = 2816
N_EXPERTS = 8
N_BRANCH = 3
EPS = 1e-6
NEG_INF = -1e30

OFF_A = 0
OFF_BQKV = OFF_A + 2 * CONV_CH
OFF_BA = OFF_BQKV + DN_QKV
OFF_BB = OFF_BA + DN_HEADS
OFF_BG = OFF_BB + DN_HEADS
OFF_C = OFF_BG + DN_HEADS * DN_DV
OFF_GATE = OFF_C + 3 * ATT_W

H_BQKV = 0
H_C = 1536
H_GATE = 3072
H_A = 6144
H_BG = 7168
H_COLS = 7680
H_BLOCK = 1536

LANES = 128
SUBLANES = 8
TILE = 512
CPT = TILE // CHUNK
FF_BLOCK = 256
VMEM_LIMIT = 48 * 1024 * 1024


def _cparams(sem):
    return pltpu.CompilerParams(dimension_semantics=sem, vmem_limit_bytes=VMEM_LIMIT)


def _split3(x):
    hi = x.astype(BF16)
    r1 = x - hi.astype(F32)
    mid = r1.astype(BF16)
    lo = (r1 - mid.astype(F32)).astype(BF16)
    return hi, mid, lo


def _dot(a, b):
    return jnp.dot(a, b, preferred_element_type=F32)


def _dot_nt(a, b):
    return lax.dot_general(a, b, (((1,), (1,)), ((), ())), preferred_element_type=F32)


def _dot_tn(a, b):
    return lax.dot_general(a, b, (((0,), (0,)), ((), ())), preferred_element_type=F32)


def _dot_sel_left(sel_bf16, x):
    hi, mid, lo = _split3(x)
    return _dot(sel_bf16, hi) + _dot(sel_bf16, mid) + _dot(sel_bf16, lo)


def _dot_sel_right(x, sel_bf16):
    hi, mid, lo = _split3(x)
    return _dot(hi, sel_bf16) + _dot(mid, sel_bf16) + _dot(lo, sel_bf16)


def _sigmoid(x):
    return 1.0 / (1.0 + jnp.exp(-x))


def _silu(x):
    return x * _sigmoid(x)


def _in_proj_kernel(x_ref, g_ref, w_ref, wab_ref, wabt_ref, h_ref, ab_ref, abt_ref, xn_ref):
    @pl.when(pl.program_id(1) == 0)
    def _():
        x = x_ref[...]
        xn = x * lax.rsqrt(jnp.mean(x * x, axis=-1, keepdims=True) + EPS) * g_ref[...]
        xnb = xn.astype(BF16)
        xn_ref[...] = xnb
        ab_ref[...] = _dot(xnb, wab_ref[...])
        abt_ref[...] = _dot_nt(wabt_ref[...], xnb)

    h_ref[...] = _dot(xn_ref[...], w_ref[...]).astype(BF16)


def _in_proj(x, g, w, wab, wabt):
    t = x.shape[0]
    return pl.pallas_call(
        _in_proj_kernel,
        grid=(t // TILE, H_COLS // H_BLOCK),
        in_specs=[
            pl.BlockSpec((TILE, D_MODEL), lambda i, j: (i, 0)),
            pl.BlockSpec((1, D_MODEL), lambda i, j: (0, 0)),
            pl.BlockSpec((D_MODEL, H_BLOCK), lambda i, j: (0, j)),
            pl.BlockSpec((D_MODEL, 2 * LANES), lambda i, j: (0, 0)),
            pl.BlockSpec((2 * SUBLANES, D_MODEL), lambda i, j: (0, 0)),
        ],
        out_specs=[
            pl.BlockSpec((TILE, H_BLOCK), lambda i, j: (i, j)),
            pl.BlockSpec((TILE, 2 * LANES), lambda i, j: (i, 0)),
            pl.BlockSpec((2 * SUBLANES, TILE), lambda i, j: (0, i)),
        ],
        out_shape=[
            jax.ShapeDtypeStruct((t, H_COLS), BF16),
            jax.ShapeDtypeStruct((t, 2 * LANES), F32),
            jax.ShapeDtypeStruct((2 * SUBLANES, t), F32),
        ],
        scratch_shapes=[pltpu.VMEM((TILE, D_MODEL), BF16)],
        compiler_params=_cparams(("parallel", "arbitrary")),
        name="in_proj",
    )(x, g, w, wab, wabt)


A_HIST = 32


def _chunk_flags(c, n_prompt_chunks):
    is_start = jnp.logical_or(c == 0, c >= n_prompt_chunks)
    is_end = c >= n_prompt_chunks - 1
    seq = jnp.maximum(c - (n_prompt_chunks - 1), 0)
    return is_start, is_end, seq


def _conv_a_kernel(h_ref, hist_ref, dw_ref, dwb_ref, lng_ref, lnb_ref, act_ref, st_ref,
                   xp_ref, cv_ref, *, n_prompt_chunks):
    i = pl.program_id(0)
    hv = h_ref[...]
    xp_ref[A_HIST:A_HIST + TILE, :] = (
        hv[:, :CONV_CH].astype(F32) * _sigmoid(hv[:, CONV_CH:].astype(F32)))
    first_tap = A_HIST - (CONV_WIDTH - 1)
    for j in range(CPT):
        base = CHUNK * j
        is_start, is_end, seq = _chunk_flags(CPT * i + j, n_prompt_chunks)

        @pl.when(is_start)
        def _():
            xp_ref[base:base + A_HIST, :] = hist_ref[seq]

        for grp in range(CONV_CH // LANES):
            cols = slice(grp * LANES, (grp + 1) * LANES)
            acc = jnp.zeros((CHUNK, LANES), F32)
            for sub in range(SUBLANES):
                taps = [k for k in range(CONV_WIDTH) if (k + first_tap) % SUBLANES == sub]
                top = max(k + first_tap for k in taps) - sub
                xb = xp_ref[base + sub:base + sub + top + CHUNK, cols]
                for k in taps:
                    off = k + first_tap - sub
                    acc = acc + xb[off:off + CHUNK, :] * dw_ref[k:k + 1, cols]
            cv_ref[base:base + CHUNK, cols] = acc

        @pl.when(is_end)
        def _():
            st_ref[seq] = xp_ref[base + CHUNK:base + CHUNK + A_HIST, :]

    xp_ref[0:A_HIST, :] = xp_ref[TILE:TILE + A_HIST, :]
    y = cv_ref[...] + dwb_ref[...]
    yc = y - jnp.mean(y, axis=-1, keepdims=True)
    yn = yc * lax.rsqrt(jnp.mean(yc * yc, axis=-1, keepdims=True) + EPS)
    act_ref[...] = _silu(yn * lng_ref[...] + lnb_ref[...]).astype(BF16)


def _conv_a(h, hist, dw, dwb, lng, lnb, n_prompt_chunks):
    t = h.shape[0]
    n_seq = hist.shape[0]
    full = lambda *shape: pl.BlockSpec(shape, lambda i: (0,) * len(shape))
    return pl.pallas_call(
        functools.partial(_conv_a_kernel, n_prompt_chunks=n_prompt_chunks),
        grid=(t // TILE,),
        in_specs=[
            pl.BlockSpec((TILE, 2 * CONV_CH), lambda i: (i, H_A // (2 * CONV_CH))),
            full(n_seq, A_HIST, CONV_CH),
            full(CONV_WIDTH, CONV_CH),
            full(1, CONV_CH), full(1, CONV_CH), full(1, CONV_CH),
        ],
        out_specs=[
            pl.BlockSpec((TILE, CONV_CH), lambda i: (i, 0)),
            full(n_seq, A_HIST, CONV_CH),
        ],
        out_shape=[
            jax.ShapeDtypeStruct((t, CONV_CH), BF16),
            jax.ShapeDtypeStruct((n_seq, A_HIST, CONV_CH), F32),
        ],
        scratch_shapes=[
            pltpu.VMEM((A_HIST + TILE, CONV_CH), F32),
            pltpu.VMEM((TILE, CONV_CH), F32),
        ],
        compiler_params=_cparams(("arbitrary",)),
        name="conv_a",
    )(h, hist, dw, dwb, lng, lnb)


B_HIST = SUBLANES


def _delta_kernel(qkv_ref, ab_ref, abt_ref, gate_ref, hist_ref, s0_ref, dw_ref,
                  alog_r_ref, dtb_r_ref, alog_c_ref, dtb_c_ref, onorm_ref,
                  ob_ref, sc_out_ref, s_out_ref,
                  xq_ref, pc_ref, gcc_ref, gcr_ref, beta_ref, s_ref, *, n_prompt_chunks):
    i = pl.program_id(0)
    first_tap = B_HIST - (DN_CONV - 1)

    xq_ref[B_HIST:B_HIST + TILE, :] = qkv_ref[...].astype(F32)
    for j in range(CPT):
        base = CHUNK * j
        is_start, is_end, seq = _chunk_flags(CPT * i + j, n_prompt_chunks)

        @pl.when(is_start)
        def _():
            xq_ref[base:base + B_HIST, :] = hist_ref[seq]

        for grp in range(DN_QKV // LANES):
            cols = slice(grp * LANES, (grp + 1) * LANES)
            acc = jnp.zeros((CHUNK, LANES), F32)
            for k in range(DN_CONV):
                off = base + first_tap + k
                acc = acc + xq_ref[off:off + CHUNK, cols] * dw_ref[k:k + 1, cols]
            pc_ref[base:base + CHUNK, cols] = acc

        @pl.when(is_end)
        def _():
            sc_out_ref[seq] = xq_ref[base + CHUNK:base + CHUNK + B_HIST, :]

    xq_ref[0:B_HIST, :] = xq_ref[TILE:TILE + B_HIST, :]

    for grp in range(DN_QKV // LANES):
        cols = slice(grp * LANES, (grp + 1) * LANES)
        x = _silu(pc_ref[:, cols])
        if grp < 2 * DN_HEADS:
            x = x * lax.rsqrt(jnp.sum(x * x, axis=-1, keepdims=True) + EPS)
        if grp < DN_HEADS:
            x = x * (DN_DK ** -0.5)
        pc_ref[:, cols] = x

    def log_decay(alpha, alog, dtb):
        z = alpha + dtb
        softplus = jnp.maximum(z, 0.0) + jnp.log(1.0 + jnp.exp(-jnp.abs(z)))
        return -jnp.exp(alog) * softplus

    ab = ab_ref[...]
    g_col = log_decay(ab[:, :LANES], alog_r_ref[...], dtb_r_ref[...])
    beta_ref[...] = _sigmoid(ab[:, LANES:])
    abt = abt_ref[...]
    g_row = log_decay(abt[:SUBLANES, :], alog_c_ref[...], dtb_c_ref[...])
    r = lax.broadcasted_iota(jnp.int32, (TILE, TILE), 0)
    c = lax.broadcasted_iota(jnp.int32, (TILE, TILE), 1)
    same_chunk = (r // CHUNK) == (c // CHUNK)
    low = jnp.where(jnp.logical_and(same_chunk, c <= r), 1.0, 0.0).astype(BF16)
    upp = jnp.where(jnp.logical_and(same_chunk, r <= c), 1.0, 0.0).astype(BF16)
    gcc_ref[...] = _dot_sel_left(low, g_col)
    gc_row = _dot_sel_right(g_row, upp)
    pad = jnp.zeros((SUBLANES, LANES - CHUNK), F32)
    for j in range(CPT):
        gcr_ref[j] = jnp.concatenate([gc_row[:, CHUNK * j:CHUNK * (j + 1)], pad], axis=1)

    ri = lax.broadcasted_iota(jnp.int32, (CHUNK, CHUNK), 0)
    ci = lax.broadcasted_iota(jnp.int32, (CHUNK, CHUNK), 1)
    eye = jnp.where(ri == ci, 1.0, 0.0).astype(F32)

    def chunk_body(j, carry):
        base = pl.multiple_of(j * CHUNK, CHUNK)
        is_start, is_end, seq = _chunk_flags(CPT * i + j, n_prompt_chunks)

        @pl.when(is_start)
        def _():
            s_ref[...] = s0_ref[seq]

        rows = pl.ds(base, CHUNK)
        gcc = gcc_ref[rows, :]
        gcr = gcr_ref[j]
        beta = beta_ref[rows, :]
        for h in range(DN_HEADS):
            q = pc_ref[rows, h * DN_DK:(h + 1) * DN_DK]
            k = pc_ref[rows, DN_QK + h * DN_DK:DN_QK + (h + 1) * DN_DK]
            v = pc_ref[rows, 2 * DN_QK + h * DN_DV:2 * DN_QK + (h + 1) * DN_DV]
            gc_c = gcc[:, h:h + 1]
            gc_r = gcr[h:h + 1, 0:CHUNK]
            gc_last = gcr[h:h + 1, CHUNK - 1:CHUNK]
            b_c = beta[:, h:h + 1]
            gam = jnp.exp(jnp.where(ri >= ci, gc_c - gc_r, -jnp.inf))
            kb = k * b_c
            kbf = k.astype(BF16)
            a_low = jnp.where(ri > ci, _dot_nt(kb.astype(BF16), kbf) * gam, 0.0)
            n_pow = -a_low
            t_inv = eye + n_pow
            for _ in range(5):
                nb = n_pow.astype(BF16)
                n_pow = _dot(nb, nb)
                t_inv = t_inv + _dot(t_inv.astype(BF16), n_pow.astype(BF16))
            tb = t_inv.astype(BF16)
            eg = jnp.exp(gc_c)
            u = _dot(tb, (v * b_c).astype(BF16))
            w = _dot(tb, (kb * eg).astype(BF16))
            qk = _dot_nt(q.astype(BF16), kbf) * gam
            qg = q * eg
            kd = k * jnp.exp(gc_last - gc_c)
            s = s_ref[h]
            sb = s.astype(BF16)
            vn = u - _dot(w.astype(BF16), sb)
            vnb = vn.astype(BF16)
            o = _dot(qg.astype(BF16), sb) + _dot(qk.astype(BF16), vnb)
            s_ref[h] = s * jnp.exp(gc_last) + _dot_tn(kd.astype(BF16), vnb)
            on = o * lax.rsqrt(jnp.mean(o * o, axis=-1, keepdims=True) + EPS) * onorm_ref[...]
            gate = gate_ref[rows, h * DN_DV:(h + 1) * DN_DV].astype(F32)
            ob_ref[rows, h * DN_DV:(h + 1) * DN_DV] = (on * _silu(gate)).astype(BF16)

        @pl.when(is_end)
        def _():
            s_out_ref[seq] = s_ref[...]

        return carry

    lax.fori_loop(0, CPT, chunk_body, 0)


def _delta(h, ab, abt, hist, s0, dw, alog_r, dtb_r, alog_c, dtb_c, onorm, n_prompt_chunks):
    t = h.shape[0]
    n_seq = hist.shape[0]
    full = lambda *shape: pl.BlockSpec(shape, lambda i: (0,) * len(shape))
    return pl.pallas_call(
        functools.partial(_delta_kernel, n_prompt_chunks=n_prompt_chunks),
        grid=(t // TILE,),
        in_specs=[
            pl.BlockSpec((TILE, DN_QKV), lambda i: (i, H_BQKV // DN_QKV)),
            pl.BlockSpec((TILE, 2 * LANES), lambda i: (i, 0)),
            pl.BlockSpec((2 * SUBLANES, TILE), lambda i: (0, i)),
            pl.BlockSpec((TILE, DN_HEADS * DN_DV), lambda i: (i, H_BG // (DN_HEADS * DN_DV))),
            full(n_seq, B_HIST, DN_QKV),
            full(n_seq, DN_HEADS, DN_DK, DN_DV),
            full(DN_CONV, DN_QKV),
            full(1, LANES), full(1, LANES), full(SUBLANES, 1), full(SUBLANES, 1),
            full(1, DN_DV),
        ],
        out_specs=[
            pl.BlockSpec((TILE, DN_HEADS * DN_DV), lambda i: (i, 0)),
            full(n_seq, B_HIST, DN_QKV),
            full(n_seq, DN_HEADS, DN_DK, DN_DV),
        ],
        out_shape=[
            jax.ShapeDtypeStruct((t, DN_HEADS * DN_DV), BF16),
            jax.ShapeDtypeStruct((n_seq, B_HIST, DN_QKV), F32),
            jax.ShapeDtypeStruct((n_seq, DN_HEADS, DN_DK, DN_DV), F32),
        ],
        scratch_shapes=[
            pltpu.VMEM((B_HIST + TILE, DN_QKV), F32),
            pltpu.VMEM((TILE, DN_QKV), F32),
            pltpu.VMEM((TILE, LANES), F32),
            pltpu.VMEM((CPT, SUBLANES, LANES), F32),
            pltpu.VMEM((TILE, LANES), F32),
            pltpu.VMEM((DN_HEADS, DN_DK, DN_DV), F32),
        ],
        compiler_params=_cparams(("arbitrary",)),
        name="delta",
    )(h, ab, abt, h, hist, s0, dw, alog_r, dtb_r, alog_c, dtb_c, onorm)


def _bias_table_kernel(rb_ref, out_ref):
    rel = lax.broadcasted_iota(jnp.int32, (2 * LANES, BAND), 0)
    key = lax.broadcasted_iota(jnp.int32, (2 * LANES, BAND), 1)
    rb = rb_ref[...]
    for a in range(CHUNK):
        idx = jnp.clip(ATT_PREV * CHUNK + a - key, REL_MIN, REL_MAX) - REL_MIN
        onehot = jnp.where(rel == idx, 1.0, 0.0).astype(BF16)
        out_ref[:, a, :] = _dot_sel_right(rb, onehot)


def _bias_table(rel_bias):
    rb = jnp.pad(rel_bias, ((0, 0), (0, 2 * LANES - N_REL)))
    return pl.pallas_call(
        _bias_table_kernel,
        out_shape=jax.ShapeDtypeStruct((ATT_HEADS, CHUNK, BAND), F32),
        compiler_params=pltpu.CompilerParams(vmem_limit_bytes=VMEM_LIMIT),
        name="bias_table",
    )(rb)


def _attend_chunk(q, kband_ref, vband_ref, row0, bias_ref, first_valid_col):
    lane = lax.broadcasted_iota(jnp.int32, (CHUNK, LANES), 1)
    col = lax.broadcasted_iota(jnp.int32, (CHUNK, BAND), 1)
    valid = col >= first_valid_col
    zero = jnp.zeros((), BF16)
    outs = []
    for hp in range(ATT_HEADS // 2):
        cols = slice(hp * LANES, (hp + 1) * LANES)
        q2 = q[:, cols]
        k2 = kband_ref[pl.ds(row0, BAND), cols]
        v2 = vband_ref[pl.ds(row0, BAND), cols]
        o_pair = []
        for half in range(2):
            in_head = (lane < ATT_HD) if half == 0 else (lane >= ATT_HD)
            s = _dot_nt(jnp.where(in_head, q2, zero), k2) * (ATT_HD ** -0.5)
            s = jnp.where(valid, s + bias_ref[2 * hp + half], NEG_INF)
            p = jnp.exp(s - jnp.max(s, axis=-1, keepdims=True))
            denom = jnp.sum(p, axis=-1, keepdims=True)
            o_pair.append(_dot(p.astype(BF16), v2) / denom)
        outs.append(jnp.where(lane < ATT_HD, o_pair[0], o_pair[1]))
    return outs


def _attn_prompt_kernel(q_ref, kp_ref, kc_ref, vp_ref, vc_ref, bias_ref, o_ref, kb_ref, vb_ref):
    i = pl.program_id(0)
    kb_ref[0:TILE, :] = kp_ref[...]
    kb_ref[TILE:2 * TILE, :] = kc_ref[...]
    vb_ref[0:TILE, :] = vp_ref[...]
    vb_ref[TILE:2 * TILE, :] = vc_ref[...]

    def chunk_body(j, carry):
        base = pl.multiple_of(j * CHUNK, CHUNK)
        rows = pl.ds(base, CHUNK)
        first_valid = jnp.maximum(ATT_PREV - (CPT * i + j), 0) * CHUNK
        outs = _attend_chunk(q_ref[rows, :], kb_ref, vb_ref, base, bias_ref, first_valid)
        for hp, o in enumerate(outs):
            o_ref[rows, hp * LANES:(hp + 1) * LANES] = o.astype(BF16)
        return carry

    lax.fori_loop(0, CPT, chunk_body, 0)


def _attn_prompt(h, bias, n_prompt_chunks):
    n_tiles = n_prompt_chunks // CPT
    qb, kb, vb = (H_C // ATT_W, H_C // ATT_W + 1, H_C // ATT_W + 2)
    prev = lambda i: jnp.maximum(i - 1, 0)
    return pl.pallas_call(
        _attn_prompt_kernel,
        grid=(n_tiles,),
        in_specs=[
            pl.BlockSpec((TILE, ATT_W), lambda i: (i, qb)),
            pl.BlockSpec((TILE, ATT_W), lambda i: (prev(i), kb)),
            pl.BlockSpec((TILE, ATT_W), lambda i: (i, kb)),
            pl.BlockSpec((TILE, ATT_W), lambda i: (prev(i), vb)),
            pl.BlockSpec((TILE, ATT_W), lambda i: (i, vb)),
            pl.BlockSpec((ATT_HEADS, CHUNK, BAND), lambda i: (0, 0, 0)),
        ],
        out_specs=pl.BlockSpec((TILE, ATT_W), lambda i: (i, 0)),
        out_shape=jax.ShapeDtypeStruct((n_tiles * TILE, ATT_W), BF16),
        scratch_shapes=[pltpu.VMEM((2 * TILE, ATT_W), BF16), pltpu.VMEM((2 * TILE, ATT_W), BF16)],
        compiler_params=_cparams(("arbitrary",)),
        name="attn_prompt",
    )(h, h, h, h, h, bias)


def _attn_sample_kernel(q_ref, k_ref, v_ref, ck_ref, cv_ref, bias_ref, o_ref, kb_ref, vb_ref):
    kb_ref[0:ATT_PREV * CHUNK, :] = ck_ref[0]
    kb_ref[ATT_PREV * CHUNK:BAND, :] = k_ref[...]
    vb_ref[0:ATT_PREV * CHUNK, :] = cv_ref[0]
    vb_ref[ATT_PREV * CHUNK:BAND, :] = v_ref[...]
    outs = _attend_chunk(q_ref[...], kb_ref, vb_ref, 0, bias_ref, 0)
    for hp, o in enumerate(outs):
        o_ref[:, hp * LANES:(hp + 1) * LANES] = o.astype(BF16)


def _attn_sample(h, cache_k, cache_v, bias, n_prompt_chunks):
    n_seq = cache_k.shape[0]
    qb, kb, vb = (H_C // ATT_W, H_C // ATT_W + 1, H_C // ATT_W + 2)
    return pl.pallas_call(
        _attn_sample_kernel,
        grid=(n_seq,),
        in_specs=[
            pl.BlockSpec((CHUNK, ATT_W), lambda b: (n_prompt_chunks + b, qb)),
            pl.BlockSpec((CHUNK, ATT_W), lambda b: (n_prompt_chunks + b, kb)),
            pl.BlockSpec((CHUNK, ATT_W), lambda b: (n_prompt_chunks + b, vb)),
            pl.BlockSpec((1, ATT_PREV * CHUNK, ATT_W), lambda b: (b, 0, 0)),
            pl.BlockSpec((1, ATT_PREV * CHUNK, ATT_W), lambda b: (b, 0, 0)),
            pl.BlockSpec((ATT_HEADS, CHUNK, BAND), lambda b: (0, 0, 0)),
        ],
        out_specs=pl.BlockSpec((CHUNK, ATT_W), lambda b: (b, 0)),
        out_shape=jax.ShapeDtypeStruct((n_seq * CHUNK, ATT_W), BF16),
        scratch_shapes=[pltpu.VMEM((BAND, ATT_W), BF16), pltpu.VMEM((BAND, ATT_W), BF16)],
        compiler_params=_cparams(("arbitrary",)),
        name="attn_sample",
    )(h, h, h, cache_k, cache_v, bias)


def _merge_kernel(x_ref, a_ref, b_ref, c_ref, gate_ref, wa_ref, wb_ref, wc_ref, wo_ref, bg_ref, o_ref):
    merged = jnp.zeros((TILE, D_MODEL), F32)
    for n, (act_ref, w_ref) in enumerate(((a_ref, wa_ref), (b_ref, wb_ref), (c_ref, wc_ref))):
        y = _dot(act_ref[...], w_ref[...])
        z = gate_ref[:, n * D_MODEL:(n + 1) * D_MODEL].astype(F32) + bg_ref[n:n + 1, :]
        merged = merged + _sigmoid(z) * y
    o_ref[...] = x_ref[...] + _dot(merged.astype(BF16), wo_ref[...])


def _merge(x, act_a, act_b, act_c, h, wa, wb, wc, wo, b_gate):
    t = x.shape[0]
    full = lambda *shape: pl.BlockSpec(shape, lambda i: (0,) * len(shape))
    act = pl.BlockSpec((TILE, CONV_CH), lambda i: (i, 0))
    return pl.pallas_call(
        _merge_kernel,
        grid=(t // TILE,),
        in_specs=[
            pl.BlockSpec((TILE, D_MODEL), lambda i: (i, 0)),
            act, act, act,
            pl.BlockSpec((TILE, N_BRANCH * D_MODEL), lambda i: (i, H_GATE // (N_BRANCH * D_MODEL))),
            full(CONV_CH, D_MODEL), full(DN_HEADS * DN_DV, D_MODEL), full(ATT_W, D_MODEL),
            full(D_MODEL, D_MODEL), full(N_BRANCH, D_MODEL),
        ],
        out_specs=pl.BlockSpec((TILE, D_MODEL), lambda i: (i, 0)),
        out_shape=jax.ShapeDtypeStruct((t, D_MODEL), F32),
        compiler_params=_cparams(("parallel",)),
        name="merge",
    )(x, act_a, act_b, act_c, h, wa, wb, wc, wo, b_gate)


def _rms(x, g):
    return x * lax.rsqrt(jnp.mean(x * x, axis=-1, keepdims=True) + EPS) * g


def _ffn_kernel(x_ref, g_ref, w1_ref, w3_ref, w2_ref, fg_ref, o_ref, xn_ref, acc_ref, *, final_norm):
    f = pl.program_id(1)

    @pl.when(f == 0)
    def _():
        xn_ref[...] = _rms(x_ref[...], g_ref[...]).astype(BF16)
        acc_ref[...] = x_ref[...]

    xn = xn_ref[...]
    hidden = _silu(_dot(xn, w1_ref[...])) * _dot(xn, w3_ref[...])
    acc_ref[...] += _dot(hidden.astype(BF16), w2_ref[...])

    @pl.when(f == pl.num_programs(1) - 1)
    def _():
        y = acc_ref[...]
        o_ref[...] = _rms(y, fg_ref[...]) if final_norm else y


def _ffn(x, g, w1, w3, w2, final_g, final_norm):
    t = x.shape[0]
    return pl.pallas_call(
        functools.partial(_ffn_kernel, final_norm=final_norm),
        grid=(t // TILE, D_FF // FF_BLOCK),
        in_specs=[
            pl.BlockSpec((TILE, D_MODEL), lambda i, f: (i, 0)),
            pl.BlockSpec((1, D_MODEL), lambda i, f: (0, 0)),
            pl.BlockSpec((D_MODEL, FF_BLOCK), lambda i, f: (0, f)),
            pl.BlockSpec((D_MODEL, FF_BLOCK), lambda i, f: (0, f)),
            pl.BlockSpec((FF_BLOCK, D_MODEL), lambda i, f: (f, 0)),
            pl.BlockSpec((1, D_MODEL), lambda i, f: (0, 0)),
        ],
        out_specs=pl.BlockSpec((TILE, D_MODEL), lambda i, f: (i, 0)),
        out_shape=jax.ShapeDtypeStruct((t, D_MODEL), F32),
        scratch_shapes=[pltpu.VMEM((TILE, D_MODEL), BF16), pltpu.VMEM((TILE, D_MODEL), F32)],
        compiler_params=_cparams(("parallel", "arbitrary")),
        name="ffn",
    )(x, g, w1, w3, w2, final_g)


def _moe_kernel(x_ref, g_ref, r_ref, w1_ref, w3_ref, w2_ref, fg_ref, o_ref,
                xn_ref, acc_ref, gates_ref, *, final_norm):
    e = pl.program_id(1)
    f = pl.program_id(2)
    lane = lax.broadcasted_iota(jnp.int32, (TILE, LANES), 1)

    @pl.when(jnp.logical_and(e == 0, f == 0))
    def _():
        xn = _rms(x_ref[...], g_ref[...])
        xn_ref[...] = xn.astype(BF16)
        acc_ref[...] = x_ref[...]
        xh, xm, xl = _split3(xn)
        rh, rm, rl = _split3(r_ref[...])
        logits = (_dot(xh, rh) + (_dot(xh, rm) + _dot(xm, rh))
                  + (_dot(xh, rl) + _dot(xm, rm) + _dot(xl, rh)))
        logits = jnp.where(lane < N_EXPERTS, logits, -jnp.inf)
        m1 = jnp.max(logits, axis=-1, keepdims=True)
        i1 = jnp.min(jnp.where(logits == m1, lane, LANES), axis=-1, keepdims=True)
        rest = jnp.where(lane == i1, -jnp.inf, logits)
        m2 = jnp.max(rest, axis=-1, keepdims=True)
        i2 = jnp.min(jnp.where(rest == m2, lane, LANES), axis=-1, keepdims=True)
        e2 = jnp.exp(m2 - m1)
        w_top = 1.0 / (1.0 + e2)
        gates_ref[...] = jnp.where(lane == i1, w_top, 0.0) + jnp.where(lane == i2, e2 * w_top, 0.0)

    ge = jnp.sum(jnp.where(lane == e, gates_ref[...], 0.0), axis=-1, keepdims=True)
    xn = xn_ref[...]
    hidden = _silu(_dot(xn, w1_ref[0])) * _dot(xn, w3_ref[0])
    acc_ref[...] += _dot((hidden * ge).astype(BF16), w2_ref[0])

    @pl.when(jnp.logical_and(e == pl.num_programs(1) - 1, f == pl.num_programs(2) - 1))
    def _():
        y = acc_ref[...]
        o_ref[...] = _rms(y, fg_ref[...]) if final_norm else y


def _moe(x, g, router, w1, w3, w2, final_g, final_norm):
    t = x.shape[0]
    return pl.pallas_call(
        functools.partial(_moe_kernel, final_norm=final_norm),
        grid=(t // TILE, N_EXPERTS, D_FF // FF_BLOCK),
        in_specs=[
            pl.BlockSpec((TILE, D_MODEL), lambda i, e, f: (i, 0)),
            pl.BlockSpec((1, D_MODEL), lambda i, e, f: (0, 0)),
            pl.BlockSpec((D_MODEL, LANES), lambda i, e, f: (0, 0)),
            pl.BlockSpec((1, D_MODEL, FF_BLOCK), lambda i, e, f: (e, 0, f)),
            pl.BlockSpec((1, D_MODEL, FF_BLOCK), lambda i, e, f: (e, 0, f)),
            pl.BlockSpec((1, FF_BLOCK, D_MODEL), lambda i, e, f: (e, f, 0)),
            pl.BlockSpec((1, D_MODEL), lambda i, e, f: (0, 0)),
        ],
        out_specs=pl.BlockSpec((TILE, D_MODEL), lambda i, e, f: (i, 0)),
        out_shape=jax.ShapeDtypeStruct((t, D_MODEL), F32),
        scratch_shapes=[
            pltpu.VMEM((TILE, D_MODEL), BF16),
            pltpu.VMEM((TILE, D_MODEL), F32),
            pltpu.VMEM((TILE, LANES), F32),
        ],
        compiler_params=_cparams(("parallel", "arbitrary", "arbitrary")),
        name="moe",
    )(x, g, router, w1, w3, w2, final_g)


def _pack_w_in(w):
    main = jnp.concatenate([
        w[:, OFF_BQKV:OFF_BQKV + DN_QKV],
        w[:, OFF_C:OFF_C + 3 * ATT_W],
        w[:, OFF_GATE:OFF_GATE + N_BRANCH * D_MODEL],
        w[:, OFF_A:OFF_A + 2 * CONV_CH],
        w[:, OFF_BG:OFF_BG + DN_HEADS * DN_DV],
    ], axis=1).astype(BF16)
    wa = w[:, OFF_BA:OFF_BA + DN_HEADS]
    wb = w[:, OFF_BB:OFF_BB + DN_HEADS]
    zc = jnp.zeros((D_MODEL, LANES - DN_HEADS), w.dtype)
    wab = jnp.concatenate([wa, zc, wb, zc], axis=1).astype(BF16)
    zr = jnp.zeros((SUBLANES - DN_HEADS, D_MODEL), w.dtype)
    wabt = jnp.concatenate([wa.T, zr, wb.T, zr], axis=0).astype(BF16)
    return main, wab, wabt


def _lane_row(v):
    return jnp.pad(v.astype(F32), (0, LANES - v.shape[0]))[None, :]


def _sublane_col(v):
    return jnp.pad(v.astype(F32), (0, SUBLANES - v.shape[0]))[:, None]


def kernel(x_prompt, x_sample, cache_conv_a, state_sconv_b, state_delta_b, cache_k_c, cache_v_c, norm1_g, w_in, b_gate, dw_a, dwb_a, ln_a_g, ln_a_b, w_a_out, dw_b, a_log, dt_bias, onorm_b, w_b_out, rel_bias, w_c_out, w_out, norm2_g, ffn_w1, ffn_w3, ffn_w2, router, moe_w1, moe_w3, moe_w2, final_norm_g):
    bp, seq_len, _ = x_prompt.shape
    n_samp, samp_len, _ = x_sample.shape
    depth = w_in.shape[0]
    assert bp == 1 and samp_len == CHUNK and seq_len % TILE == 0 and (n_samp * CHUNK) % TILE == 0
    n_prompt_chunks = seq_len // CHUNK
    n_prompt = seq_len
    keep = min(ATT_PREV * CHUNK, seq_len)
    dt = x_prompt.dtype

    x = jnp.concatenate([x_prompt.reshape(seq_len, D_MODEL), x_sample.reshape(n_samp * CHUNK, D_MODEL)], axis=0)
    fg = final_norm_g[None, :]
    states = []
    for l in range(depth):
        w_main, wab, wabt = _pack_w_in(w_in[l])
        h, ab, abt = _in_proj(x, norm1_g[l][None, :], w_main, wab, wabt)

        hist_a = jnp.pad(jnp.concatenate([jnp.zeros((1,) + cache_conv_a.shape[2:], dt), cache_conv_a[l]], axis=0),
                         ((0, 0), (A_HIST - (CONV_WIDTH - 1), 0), (0, 0)))
        act_a, st_a = _conv_a(h, hist_a, dw_a[l], dwb_a[l][None, :], ln_a_g[l][None, :], ln_a_b[l][None, :],
                              n_prompt_chunks)

        hist_b = jnp.pad(jnp.concatenate([jnp.zeros((1,) + state_sconv_b.shape[2:], dt), state_sconv_b[l]], axis=0),
                         ((0, 0), (B_HIST - (DN_CONV - 1), 0), (0, 0)))
        s0 = jnp.concatenate([jnp.zeros((1,) + state_delta_b.shape[2:], dt), state_delta_b[l]], axis=0)
        act_b, st_sc, st_s = _delta(h, ab, abt, hist_b, s0, dw_b[l], _lane_row(a_log[l]), _lane_row(dt_bias[l]),
                                    _sublane_col(a_log[l]), _sublane_col(dt_bias[l]), onorm_b[l][None, :],
                                    n_prompt_chunks)

        bias = _bias_table(rel_bias[l])
        ck = cache_k_c[l].reshape(n_samp, -1, ATT_W).astype(BF16)
        cv = cache_v_c[l].reshape(n_samp, -1, ATT_W).astype(BF16)
        act_c = jnp.concatenate([_attn_prompt(h, bias, n_prompt_chunks),
                                 _attn_sample(h, ck, cv, bias, n_prompt_chunks)], axis=0)

        x = _merge(x, act_a, act_b, act_c, h, w_a_out[l].astype(BF16), w_b_out[l].astype(BF16),
                   w_c_out[l].astype(BF16), w_out[l].astype(BF16), b_gate[l])

        last = l == depth - 1
        j = l // 2
        if l % 2 == 0:
            x = _ffn(x, norm2_g[l][None, :], ffn_w1[j].astype(BF16), ffn_w3[j].astype(BF16),
                     ffn_w2[j].astype(BF16), fg, last)
        else:
            x = _moe(x, norm2_g[l][None, :], jnp.pad(router[j], ((0, 0), (0, LANES - N_EXPERTS))),
                     moe_w1[j].astype(BF16), moe_w3[j].astype(BF16), moe_w2[j].astype(BF16), fg, last)

        k_new = h[:, H_C + ATT_W:H_C + 2 * ATT_W].astype(dt)
        v_new = h[:, H_C + 2 * ATT_W:H_C + 3 * ATT_W].astype(dt)
        states.append(dict(
            conv=st_a[:, A_HIST - (CONV_WIDTH - 1):, :],
            sconv=st_sc[:, B_HIST - (DN_CONV - 1):, :],
            delta=st_s,
            k=k_new, v=v_new))

    def stack(fn):
        return jnp.stack([fn(s) for s in states])

    heads = (ATT_HEADS, ATT_HD)
    y_prompt = x[:n_prompt].reshape(1, seq_len, D_MODEL)
    y_sample = x[n_prompt:].reshape(n_samp, CHUNK, D_MODEL)
    return (
        y_prompt, y_sample,
        stack(lambda s: s["conv"][:1]), stack(lambda s: s["sconv"][:1]), stack(lambda s: s["delta"][:1]),
        stack(lambda s: s["k"][n_prompt - keep:n_prompt].reshape(1, keep, *heads)),
        stack(lambda s: s["v"][n_prompt - keep:n_prompt].reshape(1, keep, *heads)),
        stack(lambda s: s["conv"][1:]), stack(lambda s: s["sconv"][1:]), stack(lambda s: s["delta"][1:]),
        stack(lambda s: s["k"][n_prompt:].reshape(n_samp, CHUNK, *heads)),
        stack(lambda s: s["v"][n_prompt:].reshape(n_samp, CHUNK, *heads)),
    )
```

```python
import functools

import jax
import jax.numpy as jnp
from jax import lax
from jax.experimental import pallas as pl
from jax.experimental.pallas import tpu as pltpu

F32 = jnp.float32
BF16 = jnp.bfloat16

D_MODEL = 1024
CHUNK = 64
CONV_CH = 512
CONV_WIDTH = 31
DN_HEADS = 4
DN_DK = 128
DN_DV = 128
DN_CONV = 4
DN_QK = DN_HEADS * DN_DK
DN_QKV = DN_HEADS * (2 * DN_DK + DN_DV)
ATT_HEADS = 8
ATT_HD = 64
ATT_W = ATT_HEADS * ATT_HD
ATT_PREV = 8
BAND = (ATT_PREV + 1) * CHUNK
REL_MAX = 128
REL_MIN = -(CHUNK - 1)
N_REL = REL_MAX - REL_MIN + 1
D_FF = 2816
N_EXPERTS = 8
N_BRANCH = 3
EPS = 1e-6
NEG_INF = -1e30

OFF_A = 0
OFF_BQKV = OFF_A + 2 * CONV_CH
OFF_BA = OFF_BQKV + DN_QKV
OFF_BB = OFF_BA + DN_HEADS
OFF_BG = OFF_BB + DN_HEADS
OFF_C = OFF_BG + DN_HEADS * DN_DV
OFF_GATE = OFF_C + 3 * ATT_W

H_BQKV = 0
H_C = 1536
H_GATE = 3072
H_A = 6144
H_BG = 7168
H_COLS = 7680
H_BLOCK = 1536

LANES = 128
SUBLANES = 8
TILE = 512
CPT = TILE // CHUNK
FF_BLOCK = D_FF // 2
VMEM_LIMIT = 48 * 1024 * 1024


def _cparams(sem):
    return pltpu.CompilerParams(dimension_semantics=sem, vmem_limit_bytes=VMEM_LIMIT)


def _split3(x):
    hi = x.astype(BF16)
    r1 = x - hi.astype(F32)
    mid = r1.astype(BF16)
    lo = (r1 - mid.astype(F32)).astype(BF16)
    return hi, mid, lo


def _dot(a, b):
    return jnp.dot(a, b, preferred_element_type=F32)


def _dot_nt(a, b):
    return lax.dot_general(a, b, (((1,), (1,)), ((), ())), preferred_element_type=F32)


def _dot_tn(a, b):
    return lax.dot_general(a, b, (((0,), (0,)), ((), ())), preferred_element_type=F32)


def _dot_sel_left(sel_bf16, x):
    hi, mid, lo = _split3(x)
    return _dot(sel_bf16, hi) + _dot(sel_bf16, mid) + _dot(sel_bf16, lo)


def _dot_sel_right(x, sel_bf16):
    hi, mid, lo = _split3(x)
    return _dot(hi, sel_bf16) + _dot(mid, sel_bf16) + _dot(lo, sel_bf16)


def _sigmoid(x):
    return 1.0 / (1.0 + jnp.exp(-x))


def _silu(x):
    return x * _sigmoid(x)


def _in_proj_kernel(x_ref, g_ref, w_ref, wab_ref, wabt_ref, h_ref, ab_ref, abt_ref, xn_ref):
    @pl.when(pl.program_id(1) == 0)
    def _():
        x = x_ref[...]
        xn = x * lax.rsqrt(jnp.mean(x * x, axis=-1, keepdims=True) + EPS) * g_ref[...]
        xnb = xn.astype(BF16)
        xn_ref[...] = xnb
        ab_ref[...] = _dot(xnb, wab_ref[...])
        abt_ref[...] = _dot_nt(wabt_ref[...], xnb)

    h_ref[...] = _dot(xn_ref[...], w_ref[...]).astype(BF16)


def _in_proj(x, g, w, wab, wabt):
    t = x.shape[0]
    return pl.pallas_call(
        _in_proj_kernel,
        grid=(t // TILE, H_COLS // H_BLOCK),
        in_specs=[
            pl.BlockSpec((TILE, D_MODEL), lambda i, j: (i, 0)),
            pl.BlockSpec((1, D_MODEL), lambda i, j: (0, 0)),
            pl.BlockSpec((D_MODEL, H_BLOCK), lambda i, j: (0, j)),
            pl.BlockSpec((D_MODEL, 2 * LANES), lambda i, j: (0, 0)),
            pl.BlockSpec((2 * SUBLANES, D_MODEL), lambda i, j: (0, 0)),
        ],
        out_specs=[
            pl.BlockSpec((TILE, H_BLOCK), lambda i, j: (i, j)),
            pl.BlockSpec((TILE, 2 * LANES), lambda i, j: (i, 0)),
            pl.BlockSpec((2 * SUBLANES, TILE), lambda i, j: (0, i)),
        ],
        out_shape=[
            jax.ShapeDtypeStruct((t, H_COLS), BF16),
            jax.ShapeDtypeStruct((t, 2 * LANES), F32),
            jax.ShapeDtypeStruct((2 * SUBLANES, t), F32),
        ],
        scratch_shapes=[pltpu.VMEM((TILE, D_MODEL), BF16)],
        compiler_params=_cparams(("parallel", "arbitrary")),
        name="in_proj",
    )(x, g, w, wab, wabt)


A_HIST = 32


def _chunk_flags(c, n_prompt_chunks):
    is_start = jnp.logical_or(c == 0, c >= n_prompt_chunks)
    is_end = c >= n_prompt_chunks - 1
    seq = jnp.maximum(c - (n_prompt_chunks - 1), 0)
    return is_start, is_end, seq


def _conv_a_kernel(h_ref, hist_ref, dw_ref, dwb_ref, lng_ref, lnb_ref, act_ref, st_ref,
                   xp_ref, cv_ref, *, n_prompt_chunks):
    i = pl.program_id(0)
    hv = h_ref[...]
    xp_ref[A_HIST:A_HIST + TILE, :] = (
        hv[:, :CONV_CH].astype(F32) * _sigmoid(hv[:, CONV_CH:].astype(F32)))
    first_tap = A_HIST - (CONV_WIDTH - 1)
    for j in range(CPT):
        base = CHUNK * j
        is_start, is_end, seq = _chunk_flags(CPT * i + j, n_prompt_chunks)

        @pl.when(is_start)
        def _():
            xp_ref[base:base + A_HIST, :] = hist_ref[seq]

        for grp in range(CONV_CH // LANES):
            cols = slice(grp * LANES, (grp + 1) * LANES)
            acc = jnp.zeros((CHUNK, LANES), F32)
            for sub in range(SUBLANES):
                taps = [k for k in range(CONV_WIDTH) if (k + first_tap) % SUBLANES == sub]
                top = max(k + first_tap for k in taps) - sub
                xb = xp_ref[base + sub:base + sub + top + CHUNK, cols]
                for k in taps:
                    off = k + first_tap - sub
                    acc = acc + xb[off:off + CHUNK, :] * dw_ref[k:k + 1, cols]
            cv_ref[base:base + CHUNK, cols] = acc

        @pl.when(is_end)
        def _():
            st_ref[seq] = xp_ref[base + CHUNK:base + CHUNK + A_HIST, :]

    xp_ref[0:A_HIST, :] = xp_ref[TILE:TILE + A_HIST, :]
    y = cv_ref[...] + dwb_ref[...]
    yc = y - jnp.mean(y, axis=-1, keepdims=True)
    yn = yc * lax.rsqrt(jnp.mean(yc * yc, axis=-1, keepdims=True) + EPS)
    act_ref[...] = _silu(yn * lng_ref[...] + lnb_ref[...]).astype(BF16)


def _conv_a(h, hist, dw, dwb, lng, lnb, n_prompt_chunks):
    t = h.shape[0]
    n_seq = hist.shape[0]
    full = lambda *shape: pl.BlockSpec(shape, lambda i: (0,) * len(shape))
    return pl.pallas_call(
        functools.partial(_conv_a_kernel, n_prompt_chunks=n_prompt_chunks),
        grid=(t // TILE,),
        in_specs=[
            pl.BlockSpec((TILE, 2 * CONV_CH), lambda i: (i, H_A // (2 * CONV_CH))),
            full(n_seq, A_HIST, CONV_CH),
            full(CONV_WIDTH, CONV_CH),
            full(1, CONV_CH), full(1, CONV_CH), full(1, CONV_CH),
        ],
        out_specs=[
            pl.BlockSpec((TILE, CONV_CH), lambda i: (i, 0)),
            full(n_seq, A_HIST, CONV_CH),
        ],
        out_shape=[
            jax.ShapeDtypeStruct((t, CONV_CH), BF16),
            jax.ShapeDtypeStruct((n_seq, A_HIST, CONV_CH), F32),
        ],
        scratch_shapes=[
            pltpu.VMEM((A_HIST + TILE, CONV_CH), F32),
            pltpu.VMEM((TILE, CONV_CH), F32),
        ],
        compiler_params=_cparams(("arbitrary",)),
        name="conv_a",
    )(h, hist, dw, dwb, lng, lnb)


B_HIST = SUBLANES


def _delta_kernel(qkv_ref, ab_ref, abt_ref, gate_ref, hist_ref, s0_ref, dw_ref,
                  alog_r_ref, dtb_r_ref, alog_c_ref, dtb_c_ref, onorm_ref,
                  ob_ref, sc_out_ref, s_out_ref,
                  xq_ref, pc_ref, gcc_ref, gcr_ref, beta_ref, s_ref, *, n_prompt_chunks):
    i = pl.program_id(0)
    first_tap = B_HIST - (DN_CONV - 1)

    xq_ref[B_HIST:B_HIST + TILE, :] = qkv_ref[...].astype(F32)
    for j in range(CPT):
        base = CHUNK * j
        is_start, is_end, seq = _chunk_flags(CPT * i + j, n_prompt_chunks)

        @pl.when(is_start)
        def _():
            xq_ref[base:base + B_HIST, :] = hist_ref[seq]

        for grp in range(DN_QKV // LANES):
            cols = slice(grp * LANES, (grp + 1) * LANES)
            acc = jnp.zeros((CHUNK, LANES), F32)
            for k in range(DN_CONV):
                off = base + first_tap + k
                acc = acc + xq_ref[off:off + CHUNK, cols] * dw_ref[k:k + 1, cols]
            pc_ref[base:base + CHUNK, cols] = acc

        @pl.when(is_end)
        def _():
            sc_out_ref[seq] = xq_ref[base + CHUNK:base + CHUNK + B_HIST, :]

    xq_ref[0:B_HIST, :] = xq_ref[TILE:TILE + B_HIST, :]

    for grp in range(DN_QKV // LANES):
        cols = slice(grp * LANES, (grp + 1) * LANES)
        x = _silu(pc_ref[:, cols])
        if grp < 2 * DN_HEADS:
            x = x * lax.rsqrt(jnp.sum(x * x, axis=-1, keepdims=True) + EPS)
        if grp < DN_HEADS:
            x = x * (DN_DK ** -0.5)
        pc_ref[:, cols] = x

    def log_decay(alpha, alog, dtb):
        z = alpha + dtb
        softplus = jnp.maximum(z, 0.0) + jnp.log(1.0 + jnp.exp(-jnp.abs(z)))
        return -jnp.exp(alog) * softplus

    ab = ab_ref[...]
    g_col = log_decay(ab[:, :LANES], alog_r_ref[...], dtb_r_ref[...])
    beta_ref[...] = _sigmoid(ab[:, LANES:])
    abt = abt_ref[...]
    g_row = log_decay(abt[:SUBLANES, :], alog_c_ref[...], dtb_c_ref[...])
    r = lax.broadcasted_iota(jnp.int32, (TILE, TILE), 0)
    c = lax.broadcasted_iota(jnp.int32, (TILE, TILE), 1)
    same_chunk = (r // CHUNK) == (c // CHUNK)
    low = jnp.where(jnp.logical_and(same_chunk, c <= r), 1.0, 0.0).astype(BF16)
    upp = jnp.where(jnp.logical_and(same_chunk, r <= c), 1.0, 0.0).astype(BF16)
    gcc_ref[...] = _dot_sel_left(low, g_col)
    gc_row = _dot_sel_right(g_row, upp)
    pad = jnp.zeros((SUBLANES, LANES - CHUNK), F32)
    for j in range(CPT):
        gcr_ref[j] = jnp.concatenate([gc_row[:, CHUNK * j:CHUNK * (j + 1)], pad], axis=1)

    ri = lax.broadcasted_iota(jnp.int32, (CHUNK, CHUNK), 0)
    ci = lax.broadcasted_iota(jnp.int32, (CHUNK, CHUNK), 1)
    eye = jnp.where(ri == ci, 1.0, 0.0).astype(F32)

    def chunk_body(j, carry):
        base = pl.multiple_of(j * CHUNK, CHUNK)
        is_start, is_end, seq = _chunk_flags(CPT * i + j, n_prompt_chunks)

        @pl.when(is_start)
        def _():
            s_ref[...] = s0_ref[seq]

        rows = pl.ds(base, CHUNK)
        gcc = gcc_ref[rows, :]
        gcr = gcr_ref[j]
        beta = beta_ref[rows, :]
        for h in range(DN_HEADS):
            q = pc_ref[rows, h * DN_DK:(h + 1) * DN_DK]
            k = pc_ref[rows, DN_QK + h * DN_DK:DN_QK + (h + 1) * DN_DK]
            v = pc_ref[rows, 2 * DN_QK + h * DN_DV:2 * DN_QK + (h + 1) * DN_DV]
            gc_c = gcc[:, h:h + 1]
            gc_r = gcr[h:h + 1, 0:CHUNK]
            gc_last = gcr[h:h + 1, CHUNK - 1:CHUNK]
            b_c = beta[:, h:h + 1]
            gam = jnp.exp(jnp.where(ri >= ci, gc_c - gc_r, -jnp.inf))
            kb = k * b_c
            kbf = k.astype(BF16)
            a_low = jnp.where(ri > ci, _dot_nt(kb.astype(BF16), kbf) * gam, 0.0)
            n_pow = -a_low
            t_inv = eye + n_pow
            for _ in range(5):
                nb = n_pow.astype(BF16)
                n_pow = _dot(nb, nb)
                t_inv = t_inv + _dot(t_inv.astype(BF16), n_pow.astype(BF16))
            tb = t_inv.astype(BF16)
            eg = jnp.exp(gc_c)
            u = _dot(tb, (v * b_c).astype(BF16))
            w = _dot(tb, (kb * eg).astype(BF16))
            qk = _dot_nt(q.astype(BF16), kbf) * gam
            qg = q * eg
            kd = k * jnp.exp(gc_last - gc_c)
            s = s_ref[h]
            sb = s.astype(BF16)
            vn = u - _dot(w.astype(BF16), sb)
            vnb = vn.astype(BF16)
            o = _dot(qg.astype(BF16), sb) + _dot(qk.astype(BF16), vnb)
            s_ref[h] = s * jnp.exp(gc_last) + _dot_tn(kd.astype(BF16), vnb)
            on = o * lax.rsqrt(jnp.mean(o * o, axis=-1, keepdims=True) + EPS) * onorm_ref[...]
            gate = gate_ref[rows, h * DN_DV:(h + 1) * DN_DV].astype(F32)
            ob_ref[rows, h * DN_DV:(h + 1) * DN_DV] = (on * _silu(gate)).astype(BF16)

        @pl.when(is_end)
        def _():
            s_out_ref[seq] = s_ref[...]

        return carry

    lax.fori_loop(0, CPT, chunk_body, 0)


def _delta(h, ab, abt, hist, s0, dw, alog_r, dtb_r, alog_c, dtb_c, onorm, n_prompt_chunks):
    t = h.shape[0]
    n_seq = hist.shape[0]
    full = lambda *shape: pl.BlockSpec(shape, lambda i: (0,) * len(shape))
    return pl.pallas_call(
        functools.partial(_delta_kernel, n_prompt_chunks=n_prompt_chunks),
        grid=(t // TILE,),
        in_specs=[
            pl.BlockSpec((TILE, DN_QKV), lambda i: (i, H_BQKV // DN_QKV)),
            pl.BlockSpec((TILE, 2 * LANES), lambda i: (i, 0)),
            pl.BlockSpec((2 * SUBLANES, TILE), lambda i: (0, i)),
            pl.BlockSpec((TILE, DN_HEADS * DN_DV), lambda i: (i, H_BG // (DN_HEADS * DN_DV))),
            full(n_seq, B_HIST, DN_QKV),
            full(n_seq, DN_HEADS, DN_DK, DN_DV),
            full(DN_CONV, DN_QKV),
            full(1, LANES), full(1, LANES), full(SUBLANES, 1), full(SUBLANES, 1),
            full(1, DN_DV),
        ],
        out_specs=[
            pl.BlockSpec((TILE, DN_HEADS * DN_DV), lambda i: (i, 0)),
            full(n_seq, B_HIST, DN_QKV),
            full(n_seq, DN_HEADS, DN_DK, DN_DV),
        ],
        out_shape=[
            jax.ShapeDtypeStruct((t, DN_HEADS * DN_DV), BF16),
            jax.ShapeDtypeStruct((n_seq, B_HIST, DN_QKV), F32),
            jax.ShapeDtypeStruct((n_seq, DN_HEADS, DN_DK, DN_DV), F32),
        ],
        scratch_shapes=[
            pltpu.VMEM((B_HIST + TILE, DN_QKV), F32),
            pltpu.VMEM((TILE, DN_QKV), F32),
            pltpu.VMEM((TILE, LANES), F32),
            pltpu.VMEM((CPT, SUBLANES, LANES), F32),
            pltpu.VMEM((TILE, LANES), F32),
            pltpu.VMEM((DN_HEADS, DN_DK, DN_DV), F32),
        ],
        compiler_params=_cparams(("arbitrary",)),
        name="delta",
    )(h, ab, abt, h, hist, s0, dw, alog_r, dtb_r, alog_c, dtb_c, onorm)


def _bias_table_kernel(rb_ref, out_ref):
    rel = lax.broadcasted_iota(jnp.int32, (2 * LANES, BAND), 0)
    key = lax.broadcasted_iota(jnp.int32, (2 * LANES, BAND), 1)
    rb = rb_ref[...]
    for a in range(CHUNK):
        idx = jnp.clip(ATT_PREV * CHUNK + a - key, REL_MIN, REL_MAX) - REL_MIN
        onehot = jnp.where(rel == idx, 1.0, 0.0).astype(BF16)
        out_ref[:, a, :] = _dot_sel_right(rb, onehot)


def _bias_table(rel_bias):
    rb = jnp.pad(rel_bias, ((0, 0), (0, 2 * LANES - N_REL)))
    return pl.pallas_call(
        _bias_table_kernel,
        out_shape=jax.ShapeDtypeStruct((ATT_HEADS, CHUNK, BAND), F32),
        compiler_params=pltpu.CompilerParams(vmem_limit_bytes=VMEM_LIMIT),
        name="bias_table",
    )(rb)


def _attend_chunk(q, kband_ref, vband_ref, row0, bias_ref, first_valid_col):
    lane = lax.broadcasted_iota(jnp.int32, (CHUNK, LANES), 1)
    col = lax.broadcasted_iota(jnp.int32, (CHUNK, BAND), 1)
    valid = col >= first_valid_col
    zero = jnp.zeros((), BF16)
    outs = []
    for hp in range(ATT_HEADS // 2):
        cols = slice(hp * LANES, (hp + 1) * LANES)
        q2 = q[:, cols]
        k2 = kband_ref[pl.ds(row0, BAND), cols]
        v2 = vband_ref[pl.ds(row0, BAND), cols]
        o_pair = []
        for half in range(2):
            in_head = (lane < ATT_HD) if half == 0 else (lane >= ATT_HD)
            s = _dot_nt(jnp.where(in_head, q2, zero), k2) * (ATT_HD ** -0.5)
            s = jnp.where(valid, s + bias_ref[2 * hp + half], NEG_INF)
            p = jnp.exp(s - jnp.max(s, axis=-1, keepdims=True))
            denom = jnp.sum(p, axis=-1, keepdims=True)
            o_pair.append(_dot(p.astype(BF16), v2) / denom)
        outs.append(jnp.where(lane < ATT_HD, o_pair[0], o_pair[1]))
    return outs


def _attn_prompt_kernel(q_ref, kp_ref, kc_ref, vp_ref, vc_ref, bias_ref, o_ref, kb_ref, vb_ref):
    i = pl.program_id(0)
    kb_ref[0:TILE, :] = kp_ref[...]
    kb_ref[TILE:2 * TILE, :] = kc_ref[...]
    vb_ref[0:TILE, :] = vp_ref[...]
    vb_ref[TILE:2 * TILE, :] = vc_ref[...]

    def chunk_body(j, carry):
        base = pl.multiple_of(j * CHUNK, CHUNK)
        rows = pl.ds(base, CHUNK)
        first_valid = jnp.maximum(ATT_PREV - (CPT * i + j), 0) * CHUNK
        outs = _attend_chunk(q_ref[rows, :], kb_ref, vb_ref, base, bias_ref, first_valid)
        for hp, o in enumerate(outs):
            o_ref[rows, hp * LANES:(hp + 1) * LANES] = o.astype(BF16)
        return carry

    lax.fori_loop(0, CPT, chunk_body, 0)


def _attn_prompt(h, bias, n_prompt_chunks):
    n_tiles = n_prompt_chunks // CPT
    qb, kb, vb = (H_C // ATT_W, H_C // ATT_W + 1, H_C // ATT_W + 2)
    prev = lambda i: jnp.maximum(i - 1, 0)
    return pl.pallas_call(
        _attn_prompt_kernel,
        grid=(n_tiles,),
        in_specs=[
            pl.BlockSpec((TILE, ATT_W), lambda i: (i, qb)),
            pl.BlockSpec((TILE, ATT_W), lambda i: (prev(i), kb)),
            pl.BlockSpec((TILE, ATT_W), lambda i: (i, kb)),
            pl.BlockSpec((TILE, ATT_W), lambda i: (prev(i), vb)),
            pl.BlockSpec((TILE, ATT_W), lambda i: (i, vb)),
            pl.BlockSpec((ATT_HEADS, CHUNK, BAND), lambda i: (0, 0, 0)),
        ],
        out_specs=pl.BlockSpec((TILE, ATT_W), lambda i: (i, 0)),
        out_shape=jax.ShapeDtypeStruct((n_tiles * TILE, ATT_W), BF16),
        scratch_shapes=[pltpu.VMEM((2 * TILE, ATT_W), BF16), pltpu.VMEM((2 * TILE, ATT_W), BF16)],
        compiler_params=_cparams(("arbitrary",)),
        name="attn_prompt",
    )(h, h, h, h, h, bias)


def _attn_sample_kernel(q_ref, k_ref, v_ref, ck_ref, cv_ref, bias_ref, o_ref, kb_ref, vb_ref):
    kb_ref[0:ATT_PREV * CHUNK, :] = ck_ref[0]
    kb_ref[ATT_PREV * CHUNK:BAND, :] = k_ref[...]
    vb_ref[0:ATT_PREV * CHUNK, :] = cv_ref[0]
    vb_ref[ATT_PREV * CHUNK:BAND, :] = v_ref[...]
    outs = _attend_chunk(q_ref[...], kb_ref, vb_ref, 0, bias_ref, 0)
    for hp, o in enumerate(outs):
        o_ref[:, hp * LANES:(hp + 1) * LANES] = o.astype(BF16)


def _attn_sample(h, cache_k, cache_v, bias, n_prompt_chunks):
    n_seq = cache_k.shape[0]
    qb, kb, vb = (H_C // ATT_W, H_C // ATT_W + 1, H_C // ATT_W + 2)
    return pl.pallas_call(
        _attn_sample_kernel,
        grid=(n_seq,),
        in_specs=[
            pl.BlockSpec((CHUNK, ATT_W), lambda b: (n_prompt_chunks + b, qb)),
            pl.BlockSpec((CHUNK, ATT_W), lambda b: (n_prompt_chunks + b, kb)),
            pl.BlockSpec((CHUNK, ATT_W), lambda b: (n_prompt_chunks + b, vb)),
            pl.BlockSpec((1, ATT_PREV * CHUNK, ATT_W), lambda b: (b, 0, 0)),
            pl.BlockSpec((1, ATT_PREV * CHUNK, ATT_W), lambda b: (b, 0, 0)),
            pl.BlockSpec((ATT_HEADS, CHUNK, BAND), lambda b: (0, 0, 0)),
        ],
        out_specs=pl.BlockSpec((CHUNK, ATT_W), lambda b: (b, 0)),
        out_shape=jax.ShapeDtypeStruct((n_seq * CHUNK, ATT_W), BF16),
        scratch_shapes=[pltpu.VMEM((BAND, ATT_W), BF16), pltpu.VMEM((BAND, ATT_W), BF16)],
        compiler_params=_cparams(("arbitrary",)),
        name="attn_sample",
    )(h, h, h, cache_k, cache_v, bias)


def _merge_kernel(x_ref, a_ref, b_ref, c_ref, gate_ref, wa_ref, wb_ref, wc_ref, wo_ref, bg_ref, o_ref):
    merged = jnp.zeros((TILE, D_MODEL), F32)
    for n, (act_ref, w_ref) in enumerate(((a_ref, wa_ref), (b_ref, wb_ref), (c_ref, wc_ref))):
        y = _dot(act_ref[...], w_ref[...])
        z = gate_ref[:, n * D_MODEL:(n + 1) * D_MODEL].astype(F32) + bg_ref[n:n + 1, :]
        merged = merged + _sigmoid(z) * y
    o_ref[...] = x_ref[...] + _dot(merged.astype(BF16), wo_ref[...])


def _merge(x, act_a, act_b, act_c, h, wa, wb, wc, wo, b_gate):
    t = x.shape[0]
    full = lambda *shape: pl.BlockSpec(shape, lambda i: (0,) * len(shape))
    act = pl.BlockSpec((TILE, CONV_CH), lambda i: (i, 0))
    return pl.pallas_call(
        _merge_kernel,
        grid=(t // TILE,),
        in_specs=[
            pl.BlockSpec((TILE, D_MODEL), lambda i: (i, 0)),
            act, act, act,
            pl.BlockSpec((TILE, N_BRANCH * D_MODEL), lambda i: (i, H_GATE // (N_BRANCH * D_MODEL))),
            full(CONV_CH, D_MODEL), full(DN_HEADS * DN_DV, D_MODEL), full(ATT_W, D_MODEL),
            full(D_MODEL, D_MODEL), full(N_BRANCH, D_MODEL),
        ],
        out_specs=pl.BlockSpec((TILE, D_MODEL), lambda i: (i, 0)),
        out_shape=jax.ShapeDtypeStruct((t, D_MODEL), F32),
        compiler_params=_cparams(("parallel",)),
        name="merge",
    )(x, act_a, act_b, act_c, h, wa, wb, wc, wo, b_gate)


def _rms(x, g):
    return x * lax.rsqrt(jnp.mean(x * x, axis=-1, keepdims=True) + EPS) * g


def _ffn_kernel(x_ref, g_ref, w1_ref, w3_ref, w2_ref, fg_ref, o_ref, xn_ref, *, final_norm):
    f = pl.program_id(1)

    @pl.when(f == 0)
    def _():
        xn_ref[...] = _rms(x_ref[...], g_ref[...]).astype(BF16)
        o_ref[...] = x_ref[...]

    xn = xn_ref[...]
    hidden = _silu(_dot(xn, w1_ref[...])) * _dot(xn, w3_ref[...])
    o_ref[...] += _dot(hidden.astype(BF16), w2_ref[...])

    if final_norm:
        @pl.when(f == pl.num_programs(1) - 1)
        def _():
            o_ref[...] = _rms(o_ref[...], fg_ref[...])


def _ffn(x, g, w1, w3, w2, final_g, final_norm):
    t = x.shape[0]
    return pl.pallas_call(
        functools.partial(_ffn_kernel, final_norm=final_norm),
        grid=(t // TILE, D_FF // FF_BLOCK),
        in_specs=[
            pl.BlockSpec((TILE, D_MODEL), lambda i, f: (i, 0)),
            pl.BlockSpec((1, D_MODEL), lambda i, f: (0, 0)),
            pl.BlockSpec((D_MODEL, FF_BLOCK), lambda i, f: (0, f)),
            pl.BlockSpec((D_MODEL, FF_BLOCK), lambda i, f: (0, f)),
            pl.BlockSpec((FF_BLOCK, D_MODEL), lambda i, f: (f, 0)),
            pl.BlockSpec((1, D_MODEL), lambda i, f: (0, 0)),
        ],
        out_specs=pl.BlockSpec((TILE, D_MODEL), lambda i, f: (i, 0)),
        out_shape=jax.ShapeDtypeStruct((t, D_MODEL), F32),
        scratch_shapes=[pltpu.VMEM((TILE, D_MODEL), BF16)],
        compiler_params=_cparams(("parallel", "arbitrary")),
        name="ffn",
    )(x, g, w1, w3, w2, final_g)


MOE_TILE = 768
MOE_MAIN = 256
MOE_OVER = 128


def _moe_kernel(x_ref, g_ref, r_ref, w1_ref, w3_ref, w2_ref, fg_ref, o_ref,
                xn_ref, gates_ref, key_ref, keyt_ref, cnt_ref, xc_ref, gs_ref, yc_ref, *, final_norm):
    e = pl.program_id(1)
    f = pl.program_id(2)
    last_f = pl.num_programs(2) - 1
    lane = lax.broadcasted_iota(jnp.int32, (MOE_TILE, LANES), 1)

    @pl.when(jnp.logical_and(e == 0, f == 0))
    def _():
        x = x_ref[...]
        xn = _rms(x, g_ref[...])
        xn_ref[...] = xn.astype(BF16)
        o_ref[...] = x
        xh, xm, xl = _split3(xn)
        rh, rm, rl = _split3(r_ref[...])
        logits = (_dot(xh, rh) + (_dot(xh, rm) + _dot(xm, rh))
                  + (_dot(xh, rl) + _dot(xm, rm) + _dot(xl, rh)))
        logits = jnp.where(lane < N_EXPERTS, logits, -jnp.inf)
        m1 = jnp.max(logits, axis=-1, keepdims=True)
        i1 = jnp.min(jnp.where(logits == m1, lane, LANES), axis=-1, keepdims=True)
        rest = jnp.where(lane == i1, -jnp.inf, logits)
        m2 = jnp.max(rest, axis=-1, keepdims=True)
        i2 = jnp.min(jnp.where(rest == m2, lane, LANES), axis=-1, keepdims=True)
        e2 = jnp.exp(m2 - m1)
        w_top = 1.0 / (1.0 + e2)
        gates_ref[...] = jnp.where(lane == i1, w_top, 0.0) + jnp.where(lane == i2, e2 * w_top, 0.0)
        routed = jnp.where(lane == i1, 1.0, jnp.where(lane == i2, 1.0, 0.0))
        r = lax.broadcasted_iota(jnp.int32, (MOE_TILE, MOE_TILE), 0)
        c = lax.broadcasted_iota(jnp.int32, (MOE_TILE, MOE_TILE), 1)
        before = jnp.where(c < r, 1.0, 0.0).astype(BF16)
        key = jnp.where(routed > 0.0, _dot(before, routed.astype(BF16)), -1.0)
        key_ref[...] = key
        keyt_ref[...] = key.T
        cnt_ref[...] = jnp.broadcast_to(jnp.sum(routed, axis=0, keepdims=True), (SUBLANES, LANES))

    lane1 = lax.broadcasted_iota(jnp.int32, (1, LANES), 1)
    n_routed = jnp.sum(jnp.where(lane1 == e, cnt_ref[0:1, :], 0.0)).astype(jnp.int32)
    n_over = jnp.maximum(n_routed - MOE_MAIN + (MOE_OVER - 1), 0) // MOE_OVER
    key_row = keyt_ref[pl.ds(e, 1), :]

    def block(r0, n_rows):
        rows = pl.ds(r0, n_rows)
        first = jnp.asarray(r0, jnp.int32).astype(F32)

        @pl.when(f == 0)
        def _():
            slot = lax.broadcasted_iota(jnp.int32, (n_rows, MOE_TILE), 0).astype(F32)
            pick = jnp.where(key_row - first == slot, 1.0, 0.0).astype(BF16)
            xc_ref[rows, :] = _dot(pick, xn_ref[...]).astype(BF16)
            gs_ref[rows, :] = _dot_sel_left(pick, gates_ref[...])

        xc = xc_ref[rows, :]
        lane_r = lax.broadcasted_iota(jnp.int32, (n_rows, LANES), 1)
        ge = jnp.sum(jnp.where(lane_r == e, gs_ref[rows, :], 0.0), axis=-1, keepdims=True)
        hidden = _silu(_dot(xc, w1_ref[0])) * _dot(xc, w3_ref[0])
        contrib = _dot((hidden * ge).astype(BF16), w2_ref[0])

        @pl.when(f == 0)
        def _():
            yc_ref[rows, :] = contrib

        @pl.when(f > 0)
        def _():
            yc_ref[rows, :] += contrib

        @pl.when(f == last_f)
        def _():
            key_col = jnp.sum(jnp.where(lane == e, key_ref[...], 0.0), axis=-1, keepdims=True)
            slot = lax.broadcasted_iota(jnp.int32, (MOE_TILE, n_rows), 1).astype(F32)
            place = jnp.where(key_col - first == slot, 1.0, 0.0).astype(BF16)
            o_ref[...] += _dot(place, yc_ref[rows, :].astype(BF16))

    block(0, MOE_MAIN)

    def over_body(b, carry):
        block(pl.multiple_of(MOE_MAIN + b * MOE_OVER, MOE_OVER), MOE_OVER)
        return carry

    lax.fori_loop(0, n_over, over_body, 0)

    if final_norm:
        @pl.when(jnp.logical_and(e == pl.num_programs(1) - 1, f == last_f))
        def _():
            o_ref[...] = _rms(o_ref[...], fg_ref[...])


def _moe(x, g, router, w1, w3, w2, final_g, final_norm):
    t = x.shape[0]
    return pl.pallas_call(
        functools.partial(_moe_kernel, final_norm=final_norm),
        grid=(t // MOE_TILE, N_EXPERTS, D_FF // FF_BLOCK),
        in_specs=[
            pl.BlockSpec((MOE_TILE, D_MODEL), lambda i, e, f: (i, 0)),
            pl.BlockSpec((1, D_MODEL), lambda i, e, f: (0, 0)),
            pl.BlockSpec((D_MODEL, LANES), lambda i, e, f: (0, 0)),
            pl.BlockSpec((1, D_MODEL, FF_BLOCK), lambda i, e, f: (e, 0, f)),
            pl.BlockSpec((1, D_MODEL, FF_BLOCK), lambda i, e, f: (e, 0, f)),
            pl.BlockSpec((1, FF_BLOCK, D_MODEL), lambda i, e, f: (e, f, 0)),
            pl.BlockSpec((1, D_MODEL), lambda i, e, f: (0, 0)),
        ],
        out_specs=pl.BlockSpec((MOE_TILE, D_MODEL), lambda i, e, f: (i, 0)),
        out_shape=jax.ShapeDtypeStruct((t, D_MODEL), F32),
        scratch_shapes=[
            pltpu.VMEM((MOE_TILE, D_MODEL), BF16),
            pltpu.VMEM((MOE_TILE, LANES), F32),
            pltpu.VMEM((MOE_TILE, LANES), F32),
            pltpu.VMEM((LANES, MOE_TILE), F32),
            pltpu.VMEM((SUBLANES, LANES), F32),
            pltpu.VMEM((MOE_TILE, D_MODEL), BF16),
            pltpu.VMEM((MOE_TILE, LANES), F32),
            pltpu.VMEM((MOE_TILE, D_MODEL), F32),
        ],
        compiler_params=_cparams(("parallel", "arbitrary", "arbitrary")),
        name="moe",
    )(x, g, router, w1, w3, w2, final_g)


def _pack_w_in(w):
    main = jnp.concatenate([
        w[:, OFF_BQKV:OFF_BQKV + DN_QKV],
        w[:, OFF_C:OFF_C + 3 * ATT_W],
        w[:, OFF_GATE:OFF_GATE + N_BRANCH * D_MODEL],
        w[:, OFF_A:OFF_A + 2 * CONV_CH],
        w[:, OFF_BG:OFF_BG + DN_HEADS * DN_DV],
    ], axis=1).astype(BF16)
    wa = w[:, OFF_BA:OFF_BA + DN_HEADS]
    wb = w[:, OFF_BB:OFF_BB + DN_HEADS]
    zc = jnp.zeros((D_MODEL, LANES - DN_HEADS), w.dtype)
    wab = jnp.concatenate([wa, zc, wb, zc], axis=1).astype(BF16)
    zr = jnp.zeros((SUBLANES - DN_HEADS, D_MODEL), w.dtype)
    wabt = jnp.concatenate([wa.T, zr, wb.T, zr], axis=0).astype(BF16)
    return main, wab, wabt


def _lane_row(v):
    return jnp.pad(v.astype(F32), (0, LANES - v.shape[0]))[None, :]


def _sublane_col(v):
    return jnp.pad(v.astype(F32), (0, SUBLANES - v.shape[0]))[:, None]


def kernel(x_prompt, x_sample, cache_conv_a, state_sconv_b, state_delta_b, cache_k_c, cache_v_c, norm1_g, w_in, b_gate, dw_a, dwb_a, ln_a_g, ln_a_b, w_a_out, dw_b, a_log, dt_bias, onorm_b, w_b_out, rel_bias, w_c_out, w_out, norm2_g, ffn_w1, ffn_w3, ffn_w2, router, moe_w1, moe_w3, moe_w2, final_norm_g):
    bp, seq_len, _ = x_prompt.shape
    n_samp, samp_len, _ = x_sample.shape
    depth = w_in.shape[0]
    assert bp == 1 and samp_len == CHUNK and seq_len % TILE == 0 and (n_samp * CHUNK) % TILE == 0
    assert (seq_len + n_samp * CHUNK) % MOE_TILE == 0
    n_prompt_chunks = seq_len // CHUNK
    n_prompt = seq_len
    keep = min(ATT_PREV * CHUNK, seq_len)
    dt = x_prompt.dtype

    x = jnp.concatenate([x_prompt.reshape(seq_len, D_MODEL), x_sample.reshape(n_samp * CHUNK, D_MODEL)], axis=0)
    fg = final_norm_g[None, :]
    states = []
    for l in range(depth):
        w_main, wab, wabt = _pack_w_in(w_in[l])
        h, ab, abt = _in_proj(x, norm1_g[l][None, :], w_main, wab, wabt)

        hist_a = jnp.pad(jnp.concatenate([jnp.zeros((1,) + cache_conv_a.shape[2:], dt), cache_conv_a[l]], axis=0),
                         ((0, 0), (A_HIST - (CONV_WIDTH - 1), 0), (0, 0)))
        act_a, st_a = _conv_a(h, hist_a, dw_a[l], dwb_a[l][None, :], ln_a_g[l][None, :], ln_a_b[l][None, :],
                              n_prompt_chunks)

        hist_b = jnp.pad(jnp.concatenate([jnp.zeros((1,) + state_sconv_b.shape[2:], dt), state_sconv_b[l]], axis=0),
                         ((0, 0), (B_HIST - (DN_CONV - 1), 0), (0, 0)))
        s0 = jnp.concatenate([jnp.zeros((1,) + state_delta_b.shape[2:], dt), state_delta_b[l]], axis=0)
        act_b, st_sc, st_s = _delta(h, ab, abt, hist_b, s0, dw_b[l], _lane_row(a_log[l]), _lane_row(dt_bias[l]),
                                    _sublane_col(a_log[l]), _sublane_col(dt_bias[l]), onorm_b[l][None, :],
                                    n_prompt_chunks)

        bias = _bias_table(rel_bias[l])
        ck = cache_k_c[l].reshape(n_samp, -1, ATT_W).astype(BF16)
        cv = cache_v_c[l].reshape(n_samp, -1, ATT_W).astype(BF16)
        act_c = jnp.concatenate([_attn_prompt(h, bias, n_prompt_chunks),
                                 _attn_sample(h, ck, cv, bias, n_prompt_chunks)], axis=0)

        x = _merge(x, act_a, act_b, act_c, h, w_a_out[l].astype(BF16), w_b_out[l].astype(BF16),
                   w_c_out[l].astype(BF16), w_out[l].astype(BF16), b_gate[l])

        last = l == depth - 1
        j = l // 2
        if l % 2 == 0:
            x = _ffn(x, norm2_g[l][None, :], ffn_w1[j].astype(BF16), ffn_w3[j].astype(BF16),
                     ffn_w2[j].astype(BF16), fg, last)
        else:
            x = _moe(x, norm2_g[l][None, :], jnp.pad(router[j], ((0, 0), (0, LANES - N_EXPERTS))),
                     moe_w1[j].astype(BF16), moe_w3[j].astype(BF16), moe_w2[j].astype(BF16), fg, last)

        k_new = h[:, H_C + ATT_W:H_C + 2 * ATT_W].astype(dt)
        v_new = h[:, H_C + 2 * ATT_W:H_C + 3 * ATT_W].astype(dt)
        states.append(dict(
            conv=st_a[:, A_HIST - (CONV_WIDTH - 1):, :],
            sconv=st_sc[:, B_HIST - (DN_CONV - 1):, :],
            delta=st_s,
            k=k_new, v=v_new))

    def stack(fn):
        return jnp.stack([fn(s) for s in states])

    heads = (ATT_HEADS, ATT_HD)
    y_prompt = x[:n_prompt].reshape(1, seq_len, D_MODEL)
    y_sample = x[n_prompt:].reshape(n_samp, CHUNK, D_MODEL)
    return (
        y_prompt, y_sample,
        stack(lambda s: s["conv"][:1]), stack(lambda s: s["sconv"][:1]), stack(lambda s: s["delta"][:1]),
        stack(lambda s: s["k"][n_prompt - keep:n_prompt].reshape(1, keep, *heads)),
        stack(lambda s: s["v"][n_prompt - keep:n_prompt].reshape(1, keep, *heads)),
        stack(lambda s: s["conv"][1:]), stack(lambda s: s["sconv"][1:]), stack(lambda s: s["delta"][1:]),
        stack(lambda s: s["k"][n_prompt:].reshape(n_samp, CHUNK, *heads)),
        stack(lambda s: s["v"][n_prompt:].reshape(n_samp, CHUNK, *heads)),
    )
```

```python
import functools

import jax
import jax.numpy as jnp
from jax import lax
from jax.experimental import pallas as pl
from jax.experimental.pallas import tpu as pltpu

F32 = jnp.float32
BF16 = jnp.bfloat16

D_MODEL = 1024
CHUNK = 64
CONV_CH = 512
CONV_WIDTH = 31
DN_HEADS = 4
DN_DK = 128
DN_DV = 128
DN_CONV = 4
DN_QK = DN_HEADS * DN_DK
DN_QKV = DN_HEADS * (2 * DN_DK + DN_DV)
ATT_HEADS = 8
ATT_HD = 64
ATT_W = ATT_HEADS * ATT_HD
ATT_PREV = 8
BAND = (ATT_PREV + 1) * CHUNK
REL_MAX = 128
REL_MIN = -(CHUNK - 1)
N_REL = REL_MAX - REL_MIN + 1
D_FF = 2816
N_EXPERTS = 8
N_BRANCH = 3
EPS = 1e-6
NEG_INF = -1e30

OFF_A = 0
OFF_BQKV = OFF_A + 2 * CONV_CH
OFF_BA = OFF_BQKV + DN_QKV
OFF_BB = OFF_BA + DN_HEADS
OFF_BG = OFF_BB + DN_HEADS
OFF_C = OFF_BG + DN_HEADS * DN_DV
OFF_GATE = OFF_C + 3 * ATT_W

H_BQKV = 0
H_C = 1536
H_GATE = 3072
H_A = 6144
H_BG = 7168
H_COLS = 7680
H_BLOCK = 1536

LANES = 128
SUBLANES = 8
TILE = 512
CPT = TILE // CHUNK
FF_BLOCK = D_FF // 2
VMEM_LIMIT = 48 * 1024 * 1024


def _cparams(sem):
    return pltpu.CompilerParams(dimension_semantics=sem, vmem_limit_bytes=VMEM_LIMIT)


def _split3(x):
    hi = x.astype(BF16)
    r1 = x - hi.astype(F32)
    mid = r1.astype(BF16)
    lo = (r1 - mid.astype(F32)).astype(BF16)
    return hi, mid, lo


def _dot(a, b):
    return jnp.dot(a, b, preferred_element_type=F32)


def _dot_nt(a, b):
    return lax.dot_general(a, b, (((1,), (1,)), ((), ())), preferred_element_type=F32)


def _dot_tn(a, b):
    return lax.dot_general(a, b, (((0,), (0,)), ((), ())), preferred_element_type=F32)


def _dot_sel_left(sel_bf16, x):
    hi, mid, lo = _split3(x)
    return _dot(sel_bf16, hi) + _dot(sel_bf16, mid) + _dot(sel_bf16, lo)


def _dot_sel_right(x, sel_bf16):
    hi, mid, lo = _split3(x)
    return _dot(hi, sel_bf16) + _dot(mid, sel_bf16) + _dot(lo, sel_bf16)


def _sigmoid(x):
    return 1.0 / (1.0 + jnp.exp(-x))


def _silu(x):
    return x * _sigmoid(x)


def _in_proj_kernel(x_ref, g_ref, w_ref, wab_ref, wabt_ref, h_ref, ab_ref, abt_ref):
    x = x_ref[...]
    xnb = (x * lax.rsqrt(jnp.mean(x * x, axis=-1, keepdims=True) + EPS) * g_ref[...]).astype(BF16)
    ab_ref[...] = _dot(xnb, wab_ref[...])
    abt_ref[...] = _dot_nt(wabt_ref[...], xnb)
    for j in range(H_COLS // H_BLOCK):
        cols = slice(j * H_BLOCK, (j + 1) * H_BLOCK)
        h_ref[:, cols] = _dot(xnb, w_ref[:, cols]).astype(BF16)


def _in_proj(x, g, w, wab, wabt):
    t = x.shape[0]
    once = pl.Buffered(1)
    return pl.pallas_call(
        _in_proj_kernel,
        grid=(t // TILE,),
        in_specs=[
            pl.BlockSpec((TILE, D_MODEL), lambda i: (i, 0)),
            pl.BlockSpec((1, D_MODEL), lambda i: (0, 0)),
            pl.BlockSpec((D_MODEL, H_COLS), lambda i: (0, 0), pipeline_mode=once),
            pl.BlockSpec((D_MODEL, 2 * LANES), lambda i: (0, 0)),
            pl.BlockSpec((2 * SUBLANES, D_MODEL), lambda i: (0, 0)),
        ],
        out_specs=[
            pl.BlockSpec((TILE, H_COLS), lambda i: (i, 0)),
            pl.BlockSpec((TILE, 2 * LANES), lambda i: (i, 0)),
            pl.BlockSpec((2 * SUBLANES, TILE), lambda i: (0, i)),
        ],
        out_shape=[
            jax.ShapeDtypeStruct((t, H_COLS), BF16),
            jax.ShapeDtypeStruct((t, 2 * LANES), F32),
            jax.ShapeDtypeStruct((2 * SUBLANES, t), F32),
        ],
        compiler_params=_cparams(("parallel",)),
        name="in_proj",
    )(x, g, w, wab, wabt)


A_HIST = 32


def _chunk_flags(c, n_prompt_chunks):
    is_start = jnp.logical_or(c == 0, c >= n_prompt_chunks)
    is_end = c >= n_prompt_chunks - 1
    seq = jnp.maximum(c - (n_prompt_chunks - 1), 0)
    return is_start, is_end, seq


def _conv_a_kernel(h_ref, hist_ref, dw_ref, dwb_ref, lng_ref, lnb_ref, act_ref, st_ref,
                   xp_ref, cv_ref, *, n_prompt_chunks):
    i = pl.program_id(0)
    hv = h_ref[...]
    xp_ref[A_HIST:A_HIST + TILE, :] = (
        hv[:, :CONV_CH].astype(F32) * _sigmoid(hv[:, CONV_CH:].astype(F32)))
    first_tap = A_HIST - (CONV_WIDTH - 1)
    for j in range(CPT):
        base = CHUNK * j
        is_start, is_end, seq = _chunk_flags(CPT * i + j, n_prompt_chunks)

        @pl.when(is_start)
        def _():
            xp_ref[base:base + A_HIST, :] = hist_ref[seq]

        for grp in range(CONV_CH // LANES):
            cols = slice(grp * LANES, (grp + 1) * LANES)
            acc = jnp.zeros((CHUNK, LANES), F32)
            for sub in range(SUBLANES):
                taps = [k for k in range(CONV_WIDTH) if (k + first_tap) % SUBLANES == sub]
                top = max(k + first_tap for k in taps) - sub
                xb = xp_ref[base + sub:base + sub + top + CHUNK, cols]
                for k in taps:
                    off = k + first_tap - sub
                    acc = acc + xb[off:off + CHUNK, :] * dw_ref[k:k + 1, cols]
            cv_ref[base:base + CHUNK, cols] = acc

        @pl.when(is_end)
        def _():
            st_ref[seq] = xp_ref[base + CHUNK:base + CHUNK + A_HIST, :]

    xp_ref[0:A_HIST, :] = xp_ref[TILE:TILE + A_HIST, :]
    y = cv_ref[...] + dwb_ref[...]
    yc = y - jnp.mean(y, axis=-1, keepdims=True)
    yn = yc * lax.rsqrt(jnp.mean(yc * yc, axis=-1, keepdims=True) + EPS)
    act_ref[...] = _silu(yn * lng_ref[...] + lnb_ref[...]).astype(BF16)


def _conv_a(h, hist, dw, dwb, lng, lnb, n_prompt_chunks):
    t = h.shape[0]
    n_seq = hist.shape[0]
    full = lambda *shape: pl.BlockSpec(shape, lambda i: (0,) * len(shape))
    return pl.pallas_call(
        functools.partial(_conv_a_kernel, n_prompt_chunks=n_prompt_chunks),
        grid=(t // TILE,),
        in_specs=[
            pl.BlockSpec((TILE, 2 * CONV_CH), lambda i: (i, H_A // (2 * CONV_CH))),
            full(n_seq, A_HIST, CONV_CH),
            full(CONV_WIDTH, CONV_CH),
            full(1, CONV_CH), full(1, CONV_CH), full(1, CONV_CH),
        ],
        out_specs=[
            pl.BlockSpec((TILE, CONV_CH), lambda i: (i, 0)),
            full(n_seq, A_HIST, CONV_CH),
        ],
        out_shape=[
            jax.ShapeDtypeStruct((t, CONV_CH), BF16),
            jax.ShapeDtypeStruct((n_seq, A_HIST, CONV_CH), F32),
        ],
        scratch_shapes=[
            pltpu.VMEM((A_HIST + TILE, CONV_CH), F32),
            pltpu.VMEM((TILE, CONV_CH), F32),
        ],
        compiler_params=_cparams(("arbitrary",)),
        name="conv_a",
    )(h, hist, dw, dwb, lng, lnb)


B_HIST = SUBLANES


def _delta_kernel(qkv_ref, ab_ref, abt_ref, gate_ref, hist_ref, s0_ref, dw_ref,
                  alog_r_ref, dtb_r_ref, alog_c_ref, dtb_c_ref, onorm_ref, sel_ref,
                  ob_ref, sc_out_ref, s_out_ref,
                  xq_ref, pc_ref, qb_ref, kb_ref, kbeta_ref, vb_ref, kbe_ref, qg_ref, kd_ref,
                  gcx_ref, dl_ref, u_ref, w_ref, qk_ref, s_ref, *, n_prompt_chunks):
    i = pl.program_id(0)
    first_tap = B_HIST - (DN_CONV - 1)

    @pl.when(i == 0)
    def _():
        s_ref[...] = jnp.zeros(s_ref.shape, F32)

    xq_ref[B_HIST:B_HIST + TILE, :] = qkv_ref[...].astype(F32)
    for j in range(CPT):
        base = CHUNK * j
        is_start, is_end, seq = _chunk_flags(CPT * i + j, n_prompt_chunks)

        @pl.when(is_start)
        def _():
            xq_ref[base:base + B_HIST, :] = hist_ref[seq]

        for grp in range(DN_QKV // LANES):
            cols = slice(grp * LANES, (grp + 1) * LANES)
            acc = jnp.zeros((CHUNK, LANES), F32)
            for k in range(DN_CONV):
                off = base + first_tap + k
                acc = acc + xq_ref[off:off + CHUNK, cols] * dw_ref[k:k + 1, cols]
            pc_ref[base:base + CHUNK, cols] = acc

        @pl.when(is_end)
        def _():
            sc_out_ref[seq] = xq_ref[base + CHUNK:base + CHUNK + B_HIST, :]

    xq_ref[0:B_HIST, :] = xq_ref[TILE:TILE + B_HIST, :]

    for grp in range(DN_QKV // LANES):
        cols = slice(grp * LANES, (grp + 1) * LANES)
        x = _silu(pc_ref[:, cols])
        if grp < 2 * DN_HEADS:
            x = x * lax.rsqrt(jnp.sum(x * x, axis=-1, keepdims=True) + EPS)
        if grp < DN_HEADS:
            x = x * (DN_DK ** -0.5)
        pc_ref[:, cols] = x

    def log_decay(alpha, alog, dtb):
        z = alpha + dtb
        softplus = jnp.maximum(z, 0.0) + jnp.log(1.0 + jnp.exp(-jnp.abs(z)))
        return -jnp.exp(alog) * softplus

    ab = ab_ref[...]
    g_col = log_decay(ab[:, :LANES], alog_r_ref[...], dtb_r_ref[...])
    beta = _sigmoid(ab[:, LANES:])
    abt = abt_ref[...]
    g_row = log_decay(abt[:SUBLANES, :], alog_c_ref[...], dtb_c_ref[...])
    gcc = _dot_sel_left(sel_ref[0], g_col)
    gc_row = _dot_sel_right(g_row, sel_ref[1])
    gcl = _dot_sel_left(sel_ref[2], gcc)

    def expansion(width):
        hr = lax.broadcasted_iota(jnp.int32, (LANES, DN_HEADS * width), 0)
        hc = lax.broadcasted_iota(jnp.int32, (LANES, DN_HEADS * width), 1)
        return jnp.where(hc // width == hr, 1.0, 0.0).astype(BF16)

    wide = expansion(DN_DK)
    beta_x = _dot_sel_right(beta, wide)
    eg_x = _dot_sel_right(jnp.exp(gcc), wide)
    ekd_x = _dot_sel_right(jnp.exp(gcl - gcc), wide)
    dl_ref[...] = _dot_sel_right(jnp.exp(gcl), wide)
    gcx_ref[...] = _dot_sel_right(gcc, expansion(CHUNK))

    q = pc_ref[:, 0:DN_QK]
    k = pc_ref[:, DN_QK:2 * DN_QK]
    kbeta = k * beta_x
    qb_ref[...] = q.astype(BF16)
    kb_ref[...] = k.astype(BF16)
    kbeta_ref[...] = kbeta.astype(BF16)
    vb_ref[...] = (pc_ref[:, 2 * DN_QK:] * beta_x).astype(BF16)
    kbe_ref[...] = (kbeta * eg_x).astype(BF16)
    qg_ref[...] = (q * eg_x).astype(BF16)
    kd_ref[...] = (k * ekd_x).astype(BF16)

    def block_diag(y, width):
        blk = lax.broadcasted_iota(jnp.int32, y.shape, 1) // width
        zero = jnp.zeros((), y.dtype)
        return jnp.concatenate([jnp.where(blk == h, y, zero) for h in range(DN_HEADS)], axis=0)

    row4 = lax.broadcasted_iota(jnp.int32, (CHUNK, DN_HEADS * CHUNK), 0)
    col4 = lax.broadcasted_iota(jnp.int32, (CHUNK, DN_HEADS * CHUNK), 1) % CHUNK
    eye4 = jnp.where(row4 == col4, 1.0, 0.0).astype(F32)

    chunk_rows = [slice(CHUNK * j, CHUNK * (j + 1)) for j in range(CPT)]
    n_pow, t_inv = [None] * CPT, [None] * CPT
    for j, rows in enumerate(chunk_rows):
        lhs = jnp.concatenate([kbeta_ref[rows, :], qb_ref[rows, :]], axis=0)
        both = _dot_nt(lhs, block_diag(kb_ref[rows, :], DN_DK))
        gc_r = jnp.concatenate([gc_row[h:h + 1, rows] for h in range(DN_HEADS)], axis=1)
        gam = jnp.exp(jnp.where(row4 >= col4, gcx_ref[rows, :] - gc_r, -jnp.inf))
        qk_ref[j] = (both[CHUNK:] * gam).astype(BF16)
        n_pow[j] = jnp.where(row4 > col4, -(both[:CHUNK] * gam), 0.0)
        t_inv[j] = eye4 + n_pow[j]
    for j in range(CPT):
        nb = n_pow[j].astype(BF16)
        n_pow[j] = _dot(nb, block_diag(nb, CHUNK))
    for _ in range(4):
        for j in range(CPT):
            nb = n_pow[j].astype(BF16)
            res = _dot(jnp.concatenate([nb, t_inv[j].astype(BF16)], axis=0), block_diag(nb, CHUNK))
            n_pow[j] = res[:CHUNK]
            t_inv[j] = t_inv[j] + res[CHUNK:]
    for j, rows in enumerate(chunk_rows):
        t_fin = t_inv[j] + _dot(t_inv[j].astype(BF16), block_diag(n_pow[j].astype(BF16), CHUNK))
        tb = t_fin.astype(BF16)
        u_ref[rows, :] = _dot(tb, block_diag(vb_ref[rows, :], DN_DV))
        w_ref[rows, :] = _dot(tb, block_diag(kbe_ref[rows, :], DN_DK)).astype(BF16)

    for j, rows in enumerate(chunk_rows):
        is_start, _, seq = _chunk_flags(CPT * i + j, n_prompt_chunks)
        vns, outs = [], []
        for h in range(DN_HEADS):
            hc = slice(h * DN_DK, (h + 1) * DN_DK)
            s = jnp.where(is_start, s0_ref[seq, h], s_ref[h])
            res = _dot(jnp.concatenate([w_ref[rows, hc], qg_ref[rows, hc]], axis=0), s.astype(BF16))
            vnb = (u_ref[rows, hc] - res[:CHUNK]).astype(BF16)
            s_new = s * dl_ref[CHUNK * j:CHUNK * j + 1, hc] + _dot_tn(kd_ref[rows, hc], vnb)
            s_ref[h] = s_new
            s_out_ref[seq, h] = s_new
            vns.append(vnb)
            outs.append(res[CHUNK:])
        vn_all = jnp.concatenate(vns, axis=1)
        o_all = jnp.concatenate(outs, axis=1) + _dot(qk_ref[j], block_diag(vn_all, DN_DV))
        for h in range(DN_HEADS):
            hc = slice(h * DN_DV, (h + 1) * DN_DV)
            o = o_all[:, hc]
            on = o * lax.rsqrt(jnp.mean(o * o, axis=-1, keepdims=True) + EPS) * onorm_ref[...]
            ob_ref[rows, hc] = (on * _silu(gate_ref[rows, hc].astype(F32))).astype(BF16)


def _chunk_selectors():
    r = jnp.arange(TILE)[:, None]
    c = jnp.arange(TILE)[None, :]
    same_chunk = (r // CHUNK) == (c // CHUNK)
    low = same_chunk & (c <= r)
    upp = same_chunk & (r <= c)
    last = c == (r // CHUNK) * CHUNK + (CHUNK - 1)
    return jnp.stack([low, upp, last]).astype(BF16)


def _delta(h, ab, abt, hist, s0, dw, alog_r, dtb_r, alog_c, dtb_c, onorm, n_prompt_chunks):
    t = h.shape[0]
    n_seq = hist.shape[0]
    full = lambda *shape: pl.BlockSpec(shape, lambda i: (0,) * len(shape))
    return pl.pallas_call(
        functools.partial(_delta_kernel, n_prompt_chunks=n_prompt_chunks),
        grid=(t // TILE,),
        in_specs=[
            pl.BlockSpec((TILE, DN_QKV), lambda i: (i, H_BQKV // DN_QKV)),
            pl.BlockSpec((TILE, 2 * LANES), lambda i: (i, 0)),
            pl.BlockSpec((2 * SUBLANES, TILE), lambda i: (0, i)),
            pl.BlockSpec((TILE, DN_HEADS * DN_DV), lambda i: (i, H_BG // (DN_HEADS * DN_DV))),
            full(n_seq, B_HIST, DN_QKV),
            full(n_seq, DN_HEADS, DN_DK, DN_DV),
            full(DN_CONV, DN_QKV),
            full(1, LANES), full(1, LANES), full(SUBLANES, 1), full(SUBLANES, 1),
            full(1, DN_DV),
            full(3, TILE, TILE),
        ],
        out_specs=[
            pl.BlockSpec((TILE, DN_HEADS * DN_DV), lambda i: (i, 0)),
            full(n_seq, B_HIST, DN_QKV),
            full(n_seq, DN_HEADS, DN_DK, DN_DV),
        ],
        out_shape=[
            jax.ShapeDtypeStruct((t, DN_HEADS * DN_DV), BF16),
            jax.ShapeDtypeStruct((n_seq, B_HIST, DN_QKV), F32),
            jax.ShapeDtypeStruct((n_seq, DN_HEADS, DN_DK, DN_DV), F32),
        ],
        scratch_shapes=[
            pltpu.VMEM((B_HIST + TILE, DN_QKV), F32),
            pltpu.VMEM((TILE, DN_QKV), F32),
            *[pltpu.VMEM((TILE, DN_QK), BF16) for _ in range(7)],
            pltpu.VMEM((TILE, DN_HEADS * CHUNK), F32),
            pltpu.VMEM((TILE, DN_QK), F32),
            pltpu.VMEM((TILE, DN_HEADS * DN_DV), F32),
            pltpu.VMEM((TILE, DN_QK), BF16),
            pltpu.VMEM((CPT, CHUNK, DN_HEADS * CHUNK), BF16),
            pltpu.VMEM((DN_HEADS, DN_DK, DN_DV), F32),
        ],
        compiler_params=_cparams(("arbitrary",)),
        name="delta",
    )(h, ab, abt, h, hist, s0, dw, alog_r, dtb_r, alog_c, dtb_c, onorm, _chunk_selectors())


PAIR = 2 * CHUNK
PAIR_BAND = (ATT_PREV + 2) * CHUNK


def _bias_table_kernel(rb_ref, out_ref):
    rel = lax.broadcasted_iota(jnp.int32, (2 * LANES, PAIR_BAND), 0)
    key = lax.broadcasted_iota(jnp.int32, (2 * LANES, PAIR_BAND), 1)
    key_chunk = lax.broadcasted_iota(jnp.int32, (ATT_HEADS, PAIR_BAND), 1) // CHUNK
    rb = rb_ref[...]
    for a in range(PAIR):
        idx = jnp.clip(ATT_PREV * CHUNK + a - key, REL_MIN, REL_MAX) - REL_MIN
        onehot = jnp.where(rel == idx, 1.0, 0.0).astype(BF16)
        band_pos = key_chunk - a // CHUNK
        seen = jnp.logical_and(band_pos >= 0, band_pos <= ATT_PREV)
        out_ref[:, a, :] = jnp.where(seen, _dot_sel_right(rb, onehot), NEG_INF)


def _bias_table(rel_bias):
    rb = jnp.pad(rel_bias, ((0, 0), (0, 2 * LANES - N_REL)))
    return pl.pallas_call(
        _bias_table_kernel,
        out_shape=jax.ShapeDtypeStruct((ATT_HEADS, PAIR, PAIR_BAND), F32),
        compiler_params=pltpu.CompilerParams(vmem_limit_bytes=VMEM_LIMIT),
        name="bias_table",
    )(rb)


def _attend(q, kband_ref, vband_ref, row0, n_keys, bias_of_head, first_valid_col):
    n_q = q.shape[0]
    lane = lax.broadcasted_iota(jnp.int32, (n_q, LANES), 1)
    valid = lax.broadcasted_iota(jnp.int32, (n_q, n_keys), 1) >= first_valid_col
    zero = jnp.zeros((), BF16)
    qs = q * jnp.asarray(ATT_HD ** -0.5, BF16)
    heads = range(ATT_HEADS)
    col_of = lambda h: slice((h // 2) * LANES, (h // 2 + 1) * LANES)
    s = []
    for h in heads:
        in_head = (lane < ATT_HD) if h % 2 == 0 else (lane >= ATT_HD)
        k2 = kband_ref[pl.ds(row0, n_keys), col_of(h)]
        s.append(_dot_nt(jnp.where(in_head, qs[:, col_of(h)], zero), k2))
    s = [jnp.where(valid, s[h] + bias_of_head(h), NEG_INF) for h in heads]
    top = [jnp.max(s[h], axis=-1, keepdims=True) for h in heads]
    p = [jnp.exp(s[h] - top[h]) for h in heads]
    denom = [jnp.sum(p[h], axis=-1, keepdims=True) for h in heads]
    o = [_dot(p[h].astype(BF16), vband_ref[pl.ds(row0, n_keys), col_of(h)]) for h in heads]
    o = [o[h] / denom[h] for h in heads]
    return [jnp.where(lane < ATT_HD, o[2 * hp], o[2 * hp + 1]) for hp in range(ATT_HEADS // 2)]


def _attn_prompt_kernel(q_ref, kp_ref, kc_ref, vp_ref, vc_ref, bias_ref, o_ref, kb_ref, vb_ref):
    i = pl.program_id(0)
    kb_ref[0:TILE, :] = kp_ref[...]
    kb_ref[TILE:2 * TILE, :] = kc_ref[...]
    vb_ref[0:TILE, :] = vp_ref[...]
    vb_ref[TILE:2 * TILE, :] = vc_ref[...]

    def pair_body(p, carry):
        base = pl.multiple_of(p * PAIR, PAIR)
        rows = pl.ds(base, PAIR)
        first_valid = jnp.maximum(ATT_PREV - (CPT * i + 2 * p), 0) * CHUNK
        outs = _attend(q_ref[rows, :], kb_ref, vb_ref, base, PAIR_BAND, lambda h: bias_ref[h], first_valid)
        for hp, o in enumerate(outs):
            o_ref[rows, hp * LANES:(hp + 1) * LANES] = o.astype(BF16)
        return carry

    lax.fori_loop(0, TILE // PAIR, pair_body, 0)


def _attn_prompt(h, bias, n_prompt_chunks):
    n_tiles = n_prompt_chunks // CPT
    qb, kb, vb = (H_C // ATT_W, H_C // ATT_W + 1, H_C // ATT_W + 2)
    prev = lambda i: jnp.maximum(i - 1, 0)
    return pl.pallas_call(
        _attn_prompt_kernel,
        grid=(n_tiles,),
        in_specs=[
            pl.BlockSpec((TILE, ATT_W), lambda i: (i, qb)),
            pl.BlockSpec((TILE, ATT_W), lambda i: (prev(i), kb)),
            pl.BlockSpec((TILE, ATT_W), lambda i: (i, kb)),
            pl.BlockSpec((TILE, ATT_W), lambda i: (prev(i), vb)),
            pl.BlockSpec((TILE, ATT_W), lambda i: (i, vb)),
            pl.BlockSpec((ATT_HEADS, PAIR, PAIR_BAND), lambda i: (0, 0, 0)),
        ],
        out_specs=pl.BlockSpec((TILE, ATT_W), lambda i: (i, 0)),
        out_shape=jax.ShapeDtypeStruct((n_tiles * TILE, ATT_W), BF16),
        scratch_shapes=[pltpu.VMEM((2 * TILE, ATT_W), BF16), pltpu.VMEM((2 * TILE, ATT_W), BF16)],
        compiler_params=_cparams(("arbitrary",)),
        name="attn_prompt",
    )(h, h, h, h, h, bias)


def _attn_sample_kernel(q_ref, k_ref, v_ref, ck_ref, cv_ref, bias_ref, o_ref, kb_ref, vb_ref):
    kb_ref[0:ATT_PREV * CHUNK, :] = ck_ref[0]
    kb_ref[ATT_PREV * CHUNK:BAND, :] = k_ref[...]
    vb_ref[0:ATT_PREV * CHUNK, :] = cv_ref[0]
    vb_ref[ATT_PREV * CHUNK:BAND, :] = v_ref[...]
    outs = _attend(q_ref[...], kb_ref, vb_ref, 0, BAND, lambda h: bias_ref[h, 0:CHUNK, 0:BAND], 0)
    for hp, o in enumerate(outs):
        o_ref[:, hp * LANES:(hp + 1) * LANES] = o.astype(BF16)


def _attn_sample(h, cache_k, cache_v, bias, n_prompt_chunks):
    n_seq = cache_k.shape[0]
    qb, kb, vb = (H_C // ATT_W, H_C // ATT_W + 1, H_C // ATT_W + 2)
    return pl.pallas_call(
        _attn_sample_kernel,
        grid=(n_seq,),
        in_specs=[
            pl.BlockSpec((CHUNK, ATT_W), lambda b: (n_prompt_chunks + b, qb)),
            pl.BlockSpec((CHUNK, ATT_W), lambda b: (n_prompt_chunks + b, kb)),
            pl.BlockSpec((CHUNK, ATT_W), lambda b: (n_prompt_chunks + b, vb)),
            pl.BlockSpec((1, ATT_PREV * CHUNK, ATT_W), lambda b: (b, 0, 0)),
            pl.BlockSpec((1, ATT_PREV * CHUNK, ATT_W), lambda b: (b, 0, 0)),
            pl.BlockSpec((ATT_HEADS, PAIR, PAIR_BAND), lambda b: (0, 0, 0)),
        ],
        out_specs=pl.BlockSpec((CHUNK, ATT_W), lambda b: (b, 0)),
        out_shape=jax.ShapeDtypeStruct((n_seq * CHUNK, ATT_W), BF16),
        scratch_shapes=[pltpu.VMEM((BAND, ATT_W), BF16), pltpu.VMEM((BAND, ATT_W), BF16)],
        compiler_params=_cparams(("arbitrary",)),
        name="attn_sample",
    )(h, h, h, cache_k, cache_v, bias)


def _merge_kernel(x_ref, a_ref, b_ref, c_ref, gate_ref, wa_ref, wb_ref, wc_ref, wo_ref, bg_ref, o_ref):
    merged = jnp.zeros((TILE, D_MODEL), F32)
    for n, (act_ref, w_ref) in enumerate(((a_ref, wa_ref), (b_ref, wb_ref), (c_ref, wc_ref))):
        y = _dot(act_ref[...], w_ref[...])
        z = gate_ref[:, n * D_MODEL:(n + 1) * D_MODEL].astype(F32) + bg_ref[n:n + 1, :]
        merged = merged + _sigmoid(z) * y
    o_ref[...] = x_ref[...] + _dot(merged.astype(BF16), wo_ref[...])


def _merge(x, act_a, act_b, act_c, h, wa, wb, wc, wo, b_gate):
    t = x.shape[0]
    full = lambda *shape: pl.BlockSpec(shape, lambda i: (0,) * len(shape))
    act = pl.BlockSpec((TILE, CONV_CH), lambda i: (i, 0))
    return pl.pallas_call(
        _merge_kernel,
        grid=(t // TILE,),
        in_specs=[
            pl.BlockSpec((TILE, D_MODEL), lambda i: (i, 0)),
            act, act, act,
            pl.BlockSpec((TILE, N_BRANCH * D_MODEL), lambda i: (i, H_GATE // (N_BRANCH * D_MODEL))),
            full(CONV_CH, D_MODEL), full(DN_HEADS * DN_DV, D_MODEL), full(ATT_W, D_MODEL),
            full(D_MODEL, D_MODEL), full(N_BRANCH, D_MODEL),
        ],
        out_specs=pl.BlockSpec((TILE, D_MODEL), lambda i: (i, 0)),
        out_shape=jax.ShapeDtypeStruct((t, D_MODEL), F32),
        compiler_params=_cparams(("parallel",)),
        name="merge",
    )(x, act_a, act_b, act_c, h, wa, wb, wc, wo, b_gate)


def _rms(x, g):
    return x * lax.rsqrt(jnp.mean(x * x, axis=-1, keepdims=True) + EPS) * g


def _ffn_kernel(x_ref, g_ref, w1_ref, w3_ref, w2_ref, fg_ref, o_ref, *, final_norm):
    x = x_ref[...]
    xn = _rms(x, g_ref[...]).astype(BF16)
    y = x
    for f in range(D_FF // FF_BLOCK):
        cols = slice(f * FF_BLOCK, (f + 1) * FF_BLOCK)
        hidden = _silu(_dot(xn, w1_ref[:, cols])) * _dot(xn, w3_ref[:, cols])
        y = y + _dot(hidden.astype(BF16), w2_ref[cols, :])
    o_ref[...] = _rms(y, fg_ref[...]) if final_norm else y


def _ffn(x, g, w1, w3, w2, final_g, final_norm):
    t = x.shape[0]
    once = pl.Buffered(1)
    return pl.pallas_call(
        functools.partial(_ffn_kernel, final_norm=final_norm),
        grid=(t // TILE,),
        in_specs=[
            pl.BlockSpec((TILE, D_MODEL), lambda i: (i, 0)),
            pl.BlockSpec((1, D_MODEL), lambda i: (0, 0)),
            pl.BlockSpec((D_MODEL, D_FF), lambda i: (0, 0), pipeline_mode=once),
            pl.BlockSpec((D_MODEL, D_FF), lambda i: (0, 0), pipeline_mode=once),
            pl.BlockSpec((D_FF, D_MODEL), lambda i: (0, 0), pipeline_mode=once),
            pl.BlockSpec((1, D_MODEL), lambda i: (0, 0)),
        ],
        out_specs=pl.BlockSpec((TILE, D_MODEL), lambda i: (i, 0)),
        out_shape=jax.ShapeDtypeStruct((t, D_MODEL), F32),
        compiler_params=_cparams(("parallel",)),
        name="ffn",
    )(x, g, w1, w3, w2, final_g)


MOE_TILE = 768
MOE_MAIN = 256
MOE_OVER = 128


def _moe_kernel(x_ref, g_ref, r_ref, w1_ref, w3_ref, w2_ref, fg_ref, o_ref,
                xn_ref, gates_ref, key_ref, keyt_ref, cnt_ref, xc_ref, gs_ref, yc_ref, *, final_norm):
    e = pl.program_id(1)
    f = pl.program_id(2)
    last_f = pl.num_programs(2) - 1
    lane = lax.broadcasted_iota(jnp.int32, (MOE_TILE, LANES), 1)

    @pl.when(jnp.logical_and(e == 0, f == 0))
    def _():
        x = x_ref[...]
        xn = _rms(x, g_ref[...])
        xn_ref[...] = xn.astype(BF16)
        o_ref[...] = x
        xh, xm, xl = _split3(xn)
        rh, rm, rl = _split3(r_ref[...])
        logits = (_dot(xh, rh) + (_dot(xh, rm) + _dot(xm, rh))
                  + (_dot(xh, rl) + _dot(xm, rm) + _dot(xl, rh)))
        logits = jnp.where(lane < N_EXPERTS, logits, -jnp.inf)
        m1 = jnp.max(logits, axis=-1, keepdims=True)
        i1 = jnp.min(jnp.where(logits == m1, lane, LANES), axis=-1, keepdims=True)
        rest = jnp.where(lane == i1, -jnp.inf, logits)
        m2 = jnp.max(rest, axis=-1, keepdims=True)
        i2 = jnp.min(jnp.where(rest == m2, lane, LANES), axis=-1, keepdims=True)
        e2 = jnp.exp(m2 - m1)
        w_top = 1.0 / (1.0 + e2)
        gates_ref[...] = jnp.where(lane == i1, w_top, 0.0) + jnp.where(lane == i2, e2 * w_top, 0.0)
        routed = jnp.where(lane == i1, 1.0, jnp.where(lane == i2, 1.0, 0.0))
        r = lax.broadcasted_iota(jnp.int32, (MOE_TILE, MOE_TILE), 0)
        c = lax.broadcasted_iota(jnp.int32, (MOE_TILE, MOE_TILE), 1)
        before = jnp.where(c < r, 1.0, 0.0).astype(BF16)
        key = jnp.where(routed > 0.0, _dot(before, routed.astype(BF16)), -1.0)
        key_ref[...] = key
        keyt_ref[...] = key.T
        cnt_ref[...] = jnp.broadcast_to(jnp.sum(routed, axis=0, keepdims=True), (SUBLANES, LANES))

    lane1 = lax.broadcasted_iota(jnp.int32, (1, LANES), 1)
    n_routed = jnp.sum(jnp.where(lane1 == e, cnt_ref[0:1, :], 0.0)).astype(jnp.int32)
    n_over = jnp.maximum(n_routed - MOE_MAIN + (MOE_OVER - 1), 0) // MOE_OVER
    key_row = keyt_ref[pl.ds(e, 1), :]

    def block(r0, n_rows):
        rows = pl.ds(r0, n_rows)
        first = jnp.asarray(r0, jnp.int32).astype(F32)

        @pl.when(f == 0)
        def _():
            slot = lax.broadcasted_iota(jnp.int32, (n_rows, MOE_TILE), 0).astype(F32)
            pick = jnp.where(key_row - first == slot, 1.0, 0.0).astype(BF16)
            xc_ref[rows, :] = _dot(pick, xn_ref[...]).astype(BF16)
            gs_ref[rows, :] = _dot_sel_left(pick, gates_ref[...])

        xc = xc_ref[rows, :]
        lane_r = lax.broadcasted_iota(jnp.int32, (n_rows, LANES), 1)
        ge = jnp.sum(jnp.where(lane_r == e, gs_ref[rows, :], 0.0), axis=-1, keepdims=True)
        hidden = _silu(_dot(xc, w1_ref[0])) * _dot(xc, w3_ref[0])
        contrib = _dot((hidden * ge).astype(BF16), w2_ref[0])

        @pl.when(f == 0)
        def _():
            yc_ref[rows, :] = contrib

        @pl.when(f > 0)
        def _():
            yc_ref[rows, :] += contrib

        @pl.when(f == last_f)
        def _():
            key_col = jnp.sum(jnp.where(lane == e, key_ref[...], 0.0), axis=-1, keepdims=True)
            slot = lax.broadcasted_iota(jnp.int32, (MOE_TILE, n_rows), 1).astype(F32)
            place = jnp.where(key_col - first == slot, 1.0, 0.0).astype(BF16)
            o_ref[...] += _dot(place, yc_ref[rows, :].astype(BF16))

    block(0, MOE_MAIN)

    def over_body(b, carry):
        block(pl.multiple_of(MOE_MAIN + b * MOE_OVER, MOE_OVER), MOE_OVER)
        return carry

    lax.fori_loop(0, n_over, over_body, 0)

    if final_norm:
        @pl.when(jnp.logical_and(e == pl.num_programs(1) - 1, f == last_f))
        def _():
            o_ref[...] = _rms(o_ref[...], fg_ref[...])


def _moe(x, g, router, w1, w3, w2, final_g, final_norm):
    t = x.shape[0]
    return pl.pallas_call(
        functools.partial(_moe_kernel, final_norm=final_norm),
        grid=(t // MOE_TILE, N_EXPERTS, D_FF // FF_BLOCK),
        in_specs=[
            pl.BlockSpec((MOE_TILE, D_MODEL), lambda i, e, f: (i, 0)),
            pl.BlockSpec((1, D_MODEL), lambda i, e, f: (0, 0)),
            pl.BlockSpec((D_MODEL, LANES), lambda i, e, f: (0, 0)),
            pl.BlockSpec((1, D_MODEL, FF_BLOCK), lambda i, e, f: (e, 0, f)),
            pl.BlockSpec((1, D_MODEL, FF_BLOCK), lambda i, e, f: (e, 0, f)),
            pl.BlockSpec((1, FF_BLOCK, D_MODEL), lambda i, e, f: (e, f, 0)),
            pl.BlockSpec((1, D_MODEL), lambda i, e, f: (0, 0)),
        ],
        out_specs=pl.BlockSpec((MOE_TILE, D_MODEL), lambda i, e, f: (i, 0)),
        out_shape=jax.ShapeDtypeStruct((t, D_MODEL), F32),
        scratch_shapes=[
            pltpu.VMEM((MOE_TILE, D_MODEL), BF16),
            pltpu.VMEM((MOE_TILE, LANES), F32),
            pltpu.VMEM((MOE_TILE, LANES), F32),
            pltpu.VMEM((LANES, MOE_TILE), F32),
            pltpu.VMEM((SUBLANES, LANES), F32),
            pltpu.VMEM((MOE_TILE, D_MODEL), BF16),
            pltpu.VMEM((MOE_TILE, LANES), F32),
            pltpu.VMEM((MOE_TILE, D_MODEL), F32),
        ],
        compiler_params=_cparams(("parallel", "arbitrary", "arbitrary")),
        name="moe",
    )(x, g, router, w1, w3, w2, final_g)


def _pack_w_in(w):
    main = jnp.concatenate([
        w[:, OFF_BQKV:OFF_BQKV + DN_QKV],
        w[:, OFF_C:OFF_C + 3 * ATT_W],
        w[:, OFF_GATE:OFF_GATE + N_BRANCH * D_MODEL],
        w[:, OFF_A:OFF_A + 2 * CONV_CH],
        w[:, OFF_BG:OFF_BG + DN_HEADS * DN_DV],
    ], axis=1).astype(BF16)
    wa = w[:, OFF_BA:OFF_BA + DN_HEADS]
    wb = w[:, OFF_BB:OFF_BB + DN_HEADS]
    zc = jnp.zeros((D_MODEL, LANES - DN_HEADS), w.dtype)
    wab = jnp.concatenate([wa, zc, wb, zc], axis=1).astype(BF16)
    zr = jnp.zeros((SUBLANES - DN_HEADS, D_MODEL), w.dtype)
    wabt = jnp.concatenate([wa.T, zr, wb.T, zr], axis=0).astype(BF16)
    return main, wab, wabt


def _lane_row(v):
    return jnp.pad(v.astype(F32), (0, LANES - v.shape[0]))[None, :]


def _sublane_col(v):
    return jnp.pad(v.astype(F32), (0, SUBLANES - v.shape[0]))[:, None]


def kernel(x_prompt, x_sample, cache_conv_a, state_sconv_b, state_delta_b, cache_k_c, cache_v_c, norm1_g, w_in, b_gate, dw_a, dwb_a, ln_a_g, ln_a_b, w_a_out, dw_b, a_log, dt_bias, onorm_b, w_b_out, rel_bias, w_c_out, w_out, norm2_g, ffn_w1, ffn_w3, ffn_w2, router, moe_w1, moe_w3, moe_w2, final_norm_g):
    bp, seq_len, _ = x_prompt.shape
    n_samp, samp_len, _ = x_sample.shape
    depth = w_in.shape[0]
    assert bp == 1 and samp_len == CHUNK and seq_len % TILE == 0 and (n_samp * CHUNK) % TILE == 0
    assert (seq_len + n_samp * CHUNK) % MOE_TILE == 0
    n_prompt_chunks = seq_len // CHUNK
    n_prompt = seq_len
    keep = min(ATT_PREV * CHUNK, seq_len)
    dt = x_prompt.dtype

    x = jnp.concatenate([x_prompt.reshape(seq_len, D_MODEL), x_sample.reshape(n_samp * CHUNK, D_MODEL)], axis=0)
    fg = final_norm_g[None, :]
    states = []
    for l in range(depth):
        w_main, wab, wabt = _pack_w_in(w_in[l])
        h, ab, abt = _in_proj(x, norm1_g[l][None, :], w_main, wab, wabt)

        hist_a = jnp.pad(jnp.concatenate([jnp.zeros((1,) + cache_conv_a.shape[2:], dt), cache_conv_a[l]], axis=0),
                         ((0, 0), (A_HIST - (CONV_WIDTH - 1), 0), (0, 0)))
        act_a, st_a = _conv_a(h, hist_a, dw_a[l], dwb_a[l][None, :], ln_a_g[l][None, :], ln_a_b[l][None, :],
                              n_prompt_chunks)

        hist_b = jnp.pad(jnp.concatenate([jnp.zeros((1,) + state_sconv_b.shape[2:], dt), state_sconv_b[l]], axis=0),
                         ((0, 0), (B_HIST - (DN_CONV - 1), 0), (0, 0)))
        s0 = jnp.concatenate([jnp.zeros((1,) + state_delta_b.shape[2:], dt), state_delta_b[l]], axis=0)
        act_b, st_sc, st_s = _delta(h, ab, abt, hist_b, s0, dw_b[l], _lane_row(a_log[l]), _lane_row(dt_bias[l]),
                                    _sublane_col(a_log[l]), _sublane_col(dt_bias[l]), onorm_b[l][None, :],
                                    n_prompt_chunks)

        bias = _bias_table(rel_bias[l])
        ck = cache_k_c[l].reshape(n_samp, -1, ATT_W).astype(BF16)
        cv = cache_v_c[l].reshape(n_samp, -1, ATT_W).astype(BF16)
        act_c = jnp.concatenate([_attn_prompt(h, bias, n_prompt_chunks),
                                 _attn_sample(h, ck, cv, bias, n_prompt_chunks)], axis=0)

        x = _merge(x, act_a, act_b, act_c, h, w_a_out[l].astype(BF16), w_b_out[l].astype(BF16),
                   w_c_out[l].astype(BF16), w_out[l].astype(BF16), b_gate[l])

        last = l == depth - 1
        j = l // 2
        if l % 2 == 0:
            x = _ffn(x, norm2_g[l][None, :], ffn_w1[j].astype(BF16), ffn_w3[j].astype(BF16),
                     ffn_w2[j].astype(BF16), fg, last)
        else:
            x = _moe(x, norm2_g[l][None, :], jnp.pad(router[j], ((0, 0), (0, LANES - N_EXPERTS))),
                     moe_w1[j].astype(BF16), moe_w3[j].astype(BF16), moe_w2[j].astype(BF16), fg, last)

        k_new = h[:, H_C + ATT_W:H_C + 2 * ATT_W].astype(dt)
        v_new = h[:, H_C + 2 * ATT_W:H_C + 3 * ATT_W].astype(dt)
        states.append(dict(
            conv=st_a[:, A_HIST - (CONV_WIDTH - 1):, :],
            sconv=st_sc[:, B_HIST - (DN_CONV - 1):, :],
            delta=st_s,
            k=k_new, v=v_new))

    def stack(fn):
        return jnp.stack([fn(s) for s in states])

    heads = (ATT_HEADS, ATT_HD)
    y_prompt = x[:n_prompt].reshape(1, seq_len, D_MODEL)
    y_sample = x[n_prompt:].reshape(n_samp, CHUNK, D_MODEL)
    return (
        y_prompt, y_sample,
        stack(lambda s: s["conv"][:1]), stack(lambda s: s["sconv"][:1]), stack(lambda s: s["delta"][:1]),
        stack(lambda s: s["k"][n_prompt - keep:n_prompt].reshape(1, keep, *heads)),
        stack(lambda s: s["v"][n_prompt - keep:n_prompt].reshape(1, keep, *heads)),
        stack(lambda s: s["conv"][1:]), stack(lambda s: s["sconv"][1:]), stack(lambda s: s["delta"][1:]),
        stack(lambda s: s["k"][n_prompt:].reshape(n_samp, CHUNK, *heads)),
        stack(lambda s: s["v"][n_prompt:].reshape(n_samp, CHUNK, *heads)),
    )
```

```python
import functools

import jax
import jax.numpy as jnp
from jax import lax
from jax.experimental import pallas as pl
from jax.experimental.pallas import tpu as pltpu

F32 = jnp.float32
BF16 = jnp.bfloat16

D_MODEL = 1024
CHUNK = 64
CONV_CH = 512
CONV_WIDTH = 31
DN_HEADS = 4
DN_DK = 128
DN_DV = 128
DN_CONV = 4
DN_QK = DN_HEADS * DN_DK
DN_QKV = DN_HEADS * (2 * DN_DK + DN_DV)
ATT_HEADS = 8
ATT_HD = 64
ATT_W = ATT_HEADS * ATT_HD
ATT_PREV = 8
BAND = (ATT_PREV + 1) * CHUNK
REL_MAX = 128
REL_MIN = -(CHUNK - 1)
N_REL = REL_MAX - REL_MIN + 1
D_FF = 2816
N_EXPERTS = 8
N_BRANCH = 3
EPS = 1e-6
NEG_INF = -1e30

OFF_A = 0
OFF_BQKV = OFF_A + 2 * CONV_CH
OFF_BA = OFF_BQKV + DN_QKV
OFF_BB = OFF_BA + DN_HEADS
OFF_BG = OFF_BB + DN_HEADS
OFF_C = OFF_BG + DN_HEADS * DN_DV
OFF_GATE = OFF_C + 3 * ATT_W

H_BQKV = 0
H_C = 1536
H_GATE = 3072
H_A = 6144
H_BG = 7168
H_COLS = 7680
H_BLOCK = 1536

LANES = 128
SUBLANES = 8
TILE = 512
CPT = TILE // CHUNK
FF_BLOCK = D_FF // 2
VMEM_LIMIT = 48 * 1024 * 1024


def _cparams(sem):
    return pltpu.CompilerParams(dimension_semantics=sem, vmem_limit_bytes=VMEM_LIMIT)


def _split3(x):
    hi = x.astype(BF16)
    r1 = x - hi.astype(F32)
    mid = r1.astype(BF16)
    lo = (r1 - mid.astype(F32)).astype(BF16)
    return hi, mid, lo


def _dot(a, b):
    return jnp.dot(a, b, preferred_element_type=F32)


def _dot_nt(a, b):
    return lax.dot_general(a, b, (((1,), (1,)), ((), ())), preferred_element_type=F32)


def _dot_tn(a, b):
    return lax.dot_general(a, b, (((0,), (0,)), ((), ())), preferred_element_type=F32)


def _dot_sel_left(sel_bf16, x):
    hi, mid, lo = _split3(x)
    return _dot(sel_bf16, hi) + _dot(sel_bf16, mid) + _dot(sel_bf16, lo)


def _dot_sel_right(x, sel_bf16):
    hi, mid, lo = _split3(x)
    return _dot(hi, sel_bf16) + _dot(mid, sel_bf16) + _dot(lo, sel_bf16)


def _sigmoid(x):
    return 1.0 / (1.0 + jnp.exp(-x))


def _silu(x):
    return x * _sigmoid(x)


def _in_proj_kernel(x_ref, g_ref, w_ref, wab_ref, wabt_ref, h_ref, ab_ref, abt_ref):
    x = x_ref[...]
    xnb = (x * lax.rsqrt(jnp.mean(x * x, axis=-1, keepdims=True) + EPS) * g_ref[...]).astype(BF16)
    ab_ref[...] = _dot(xnb, wab_ref[...])
    abt_ref[...] = _dot_nt(wabt_ref[...], xnb)
    for j in range(H_COLS // H_BLOCK):
        cols = slice(j * H_BLOCK, (j + 1) * H_BLOCK)
        h_ref[:, cols] = _dot(xnb, w_ref[:, cols]).astype(BF16)


def _in_proj(x, g, w, wab, wabt):
    t = x.shape[0]
    once = pl.Buffered(1)
    return pl.pallas_call(
        _in_proj_kernel,
        grid=(t // TILE,),
        in_specs=[
            pl.BlockSpec((TILE, D_MODEL), lambda i: (i, 0)),
            pl.BlockSpec((1, D_MODEL), lambda i: (0, 0)),
            pl.BlockSpec((D_MODEL, H_COLS), lambda i: (0, 0), pipeline_mode=once),
            pl.BlockSpec((D_MODEL, 2 * LANES), lambda i: (0, 0)),
            pl.BlockSpec((2 * SUBLANES, D_MODEL), lambda i: (0, 0)),
        ],
        out_specs=[
            pl.BlockSpec((TILE, H_COLS), lambda i: (i, 0)),
            pl.BlockSpec((TILE, 2 * LANES), lambda i: (i, 0)),
            pl.BlockSpec((2 * SUBLANES, TILE), lambda i: (0, i)),
        ],
        out_shape=[
            jax.ShapeDtypeStruct((t, H_COLS), BF16),
            jax.ShapeDtypeStruct((t, 2 * LANES), F32),
            jax.ShapeDtypeStruct((2 * SUBLANES, t), F32),
        ],
        compiler_params=_cparams(("parallel",)),
        name="in_proj",
    )(x, g, w, wab, wabt)


A_HIST = 32


def _chunk_flags(c, n_prompt_chunks):
    is_start = jnp.logical_or(c == 0, c >= n_prompt_chunks)
    is_end = c >= n_prompt_chunks - 1
    seq = jnp.maximum(c - (n_prompt_chunks - 1), 0)
    return is_start, is_end, seq


def _conv_a_kernel(h_ref, hist_ref, dw_ref, dwb_ref, lng_ref, lnb_ref, act_ref, st_ref,
                   xp_ref, cv_ref, *, n_prompt_chunks):
    i = pl.program_id(0)
    hv = h_ref[...]
    xp_ref[A_HIST:A_HIST + TILE, :] = (
        hv[:, :CONV_CH].astype(F32) * _sigmoid(hv[:, CONV_CH:].astype(F32)))
    first_tap = A_HIST - (CONV_WIDTH - 1)
    for j in range(CPT):
        base = CHUNK * j
        is_start, is_end, seq = _chunk_flags(CPT * i + j, n_prompt_chunks)

        @pl.when(is_start)
        def _():
            xp_ref[base:base + A_HIST, :] = hist_ref[seq]

        for grp in range(CONV_CH // LANES):
            cols = slice(grp * LANES, (grp + 1) * LANES)
            y = None
            for b in range(SUBLANES):
                n_rows = CHUNK if b == 0 else CHUNK + SUBLANES
                z = None
                for a in range((first_tap + CONV_WIDTH - 1) // SUBLANES + 1):
                    k = SUBLANES * a + b - first_tap
                    if 0 <= k < CONV_WIDTH:
                        row0 = base + SUBLANES * a
                        term = xp_ref[row0:row0 + n_rows, cols] * dw_ref[k:k + 1, cols]
                        z = term if z is None else z + term
                zb = z[b:b + CHUNK]
                y = zb if y is None else y + zb
            cv_ref[base:base + CHUNK, cols] = y

        @pl.when(is_end)
        def _():
            st_ref[seq] = xp_ref[base + CHUNK:base + CHUNK + A_HIST, :]

    xp_ref[0:A_HIST, :] = xp_ref[TILE:TILE + A_HIST, :]
    y = cv_ref[...] + dwb_ref[...]
    yc = y - jnp.mean(y, axis=-1, keepdims=True)
    yn = yc * lax.rsqrt(jnp.mean(yc * yc, axis=-1, keepdims=True) + EPS)
    act_ref[...] = _silu(yn * lng_ref[...] + lnb_ref[...]).astype(BF16)


def _conv_a(h, hist, dw, dwb, lng, lnb, n_prompt_chunks):
    t = h.shape[0]
    n_seq = hist.shape[0]
    full = lambda *shape: pl.BlockSpec(shape, lambda i: (0,) * len(shape))
    return pl.pallas_call(
        functools.partial(_conv_a_kernel, n_prompt_chunks=n_prompt_chunks),
        grid=(t // TILE,),
        in_specs=[
            pl.BlockSpec((TILE, 2 * CONV_CH), lambda i: (i, H_A // (2 * CONV_CH))),
            full(n_seq, A_HIST, CONV_CH),
            full(CONV_WIDTH, CONV_CH),
            full(1, CONV_CH), full(1, CONV_CH), full(1, CONV_CH),
        ],
        out_specs=[
            pl.BlockSpec((TILE, CONV_CH), lambda i: (i, 0)),
            full(n_seq, A_HIST, CONV_CH),
        ],
        out_shape=[
            jax.ShapeDtypeStruct((t, CONV_CH), BF16),
            jax.ShapeDtypeStruct((n_seq, A_HIST, CONV_CH), F32),
        ],
        scratch_shapes=[
            pltpu.VMEM((A_HIST + TILE, CONV_CH), F32),
            pltpu.VMEM((TILE, CONV_CH), F32),
        ],
        compiler_params=_cparams(("arbitrary",)),
        name="conv_a",
    )(h, hist, dw, dwb, lng, lnb)


B_HIST = SUBLANES


def _delta_kernel(qkv_ref, ab_ref, abt_ref, gate_ref, hist_ref, s0_ref, dw_ref,
                  alog_r_ref, dtb_r_ref, alog_c_ref, dtb_c_ref, onorm_ref, sel_ref,
                  ob_ref, sc_out_ref, s_out_ref,
                  xq_ref, pc_ref, qb_ref, kb_ref, kbeta_ref, vb_ref, kbe_ref, qg_ref, kd_ref,
                  gcx_ref, dl_ref, u_ref, w_ref, qk_ref, s_ref, *, n_prompt_chunks):
    i = pl.program_id(0)
    first_tap = B_HIST - (DN_CONV - 1)

    @pl.when(i == 0)
    def _():
        s_ref[...] = jnp.zeros(s_ref.shape, F32)

    xq_ref[B_HIST:B_HIST + TILE, :] = qkv_ref[...].astype(F32)
    for j in range(CPT):
        base = CHUNK * j
        is_start, is_end, seq = _chunk_flags(CPT * i + j, n_prompt_chunks)

        @pl.when(is_start)
        def _():
            xq_ref[base:base + B_HIST, :] = hist_ref[seq]

        for grp in range(DN_QKV // LANES):
            cols = slice(grp * LANES, (grp + 1) * LANES)
            acc = jnp.zeros((CHUNK, LANES), F32)
            for k in range(DN_CONV):
                off = base + first_tap + k
                acc = acc + xq_ref[off:off + CHUNK, cols] * dw_ref[k:k + 1, cols]
            pc_ref[base:base + CHUNK, cols] = acc

        @pl.when(is_end)
        def _():
            sc_out_ref[seq] = xq_ref[base + CHUNK:base + CHUNK + B_HIST, :]

    xq_ref[0:B_HIST, :] = xq_ref[TILE:TILE + B_HIST, :]

    for grp in range(DN_QKV // LANES):
        cols = slice(grp * LANES, (grp + 1) * LANES)
        x = _silu(pc_ref[:, cols])
        if grp < 2 * DN_HEADS:
            x = x * lax.rsqrt(jnp.sum(x * x, axis=-1, keepdims=True) + EPS)
        if grp < DN_HEADS:
            x = x * (DN_DK ** -0.5)
        pc_ref[:, cols] = x

    def log_decay(alpha, alog, dtb):
        z = alpha + dtb
        softplus = jnp.maximum(z, 0.0) + jnp.log(1.0 + jnp.exp(-jnp.abs(z)))
        return -jnp.exp(alog) * softplus

    ab = ab_ref[...]
    g_col = log_decay(ab[:, :LANES], alog_r_ref[...], dtb_r_ref[...])
    beta = _sigmoid(ab[:, LANES:])
    abt = abt_ref[...]
    g_row = log_decay(abt[:SUBLANES, :], alog_c_ref[...], dtb_c_ref[...])
    gcc = _dot_sel_left(sel_ref[0], g_col)
    gc_row = _dot_sel_right(g_row, sel_ref[1])
    gcl = _dot_sel_left(sel_ref[2], gcc)

    def expansion(width):
        hr = lax.broadcasted_iota(jnp.int32, (LANES, DN_HEADS * width), 0)
        hc = lax.broadcasted_iota(jnp.int32, (LANES, DN_HEADS * width), 1)
        return jnp.where(hc // width == hr, 1.0, 0.0).astype(BF16)

    wide = expansion(DN_DK)
    beta_x = _dot_sel_right(beta, wide)
    eg_x = _dot_sel_right(jnp.exp(gcc), wide)
    ekd_x = _dot_sel_right(jnp.exp(gcl - gcc), wide)
    dl_ref[...] = _dot_sel_right(jnp.exp(gcl), wide)
    gcx_ref[...] = _dot_sel_right(gcc, expansion(CHUNK))

    q = pc_ref[:, 0:DN_QK]
    k = pc_ref[:, DN_QK:2 * DN_QK]
    kbeta = k * beta_x
    qb_ref[...] = q.astype(BF16)
    kb_ref[...] = k.astype(BF16)
    kbeta_ref[...] = kbeta.astype(BF16)
    vb_ref[...] = (pc_ref[:, 2 * DN_QK:] * beta_x).astype(BF16)
    kbe_ref[...] = (kbeta * eg_x).astype(BF16)
    qg_ref[...] = (q * eg_x).astype(BF16)
    kd_ref[...] = (k * ekd_x).astype(BF16)

    def block_diag(y, width):
        blk = lax.broadcasted_iota(jnp.int32, y.shape, 1) // width
        zero = jnp.zeros((), y.dtype)
        return jnp.concatenate([jnp.where(blk == h, y, zero) for h in range(DN_HEADS)], axis=0)

    row4 = lax.broadcasted_iota(jnp.int32, (CHUNK, DN_HEADS * CHUNK), 0)
    col4 = lax.broadcasted_iota(jnp.int32, (CHUNK, DN_HEADS * CHUNK), 1) % CHUNK
    eye4 = jnp.where(row4 == col4, 1.0, 0.0).astype(F32)

    chunk_rows = [slice(CHUNK * j, CHUNK * (j + 1)) for j in range(CPT)]
    n_pow, t_inv = [None] * CPT, [None] * CPT
    for j, rows in enumerate(chunk_rows):
        lhs = jnp.concatenate([kbeta_ref[rows, :], qb_ref[rows, :]], axis=0)
        both = _dot_nt(lhs, block_diag(kb_ref[rows, :], DN_DK))
        gc_r = jnp.concatenate([gc_row[h:h + 1, rows] for h in range(DN_HEADS)], axis=1)
        gam = jnp.exp(jnp.where(row4 >= col4, gcx_ref[rows, :] - gc_r, -jnp.inf))
        qk_ref[j] = (both[CHUNK:] * gam).astype(BF16)
        n_pow[j] = jnp.where(row4 > col4, -(both[:CHUNK] * gam), 0.0)
        t_inv[j] = eye4 + n_pow[j]
    for j in range(CPT):
        nb = n_pow[j].astype(BF16)
        n_pow[j] = _dot(nb, block_diag(nb, CHUNK))
    for _ in range(4):
        for j in range(CPT):
            nb = n_pow[j].astype(BF16)
            res = _dot(jnp.concatenate([nb, t_inv[j].astype(BF16)], axis=0), block_diag(nb, CHUNK))
            n_pow[j] = res[:CHUNK]
            t_inv[j] = t_inv[j] + res[CHUNK:]
    for j, rows in enumerate(chunk_rows):
        t_fin = t_inv[j] + _dot(t_inv[j].astype(BF16), block_diag(n_pow[j].astype(BF16), CHUNK))
        tb = t_fin.astype(BF16)
        u_ref[rows, :] = _dot(tb, block_diag(vb_ref[rows, :], DN_DV))
        w_ref[rows, :] = _dot(tb, block_diag(kbe_ref[rows, :], DN_DK)).astype(BF16)

    for j, rows in enumerate(chunk_rows):
        is_start, _, seq = _chunk_flags(CPT * i + j, n_prompt_chunks)
        vns, outs = [], []
        for h in range(DN_HEADS):
            hc = slice(h * DN_DK, (h + 1) * DN_DK)
            s = jnp.where(is_start, s0_ref[seq, h], s_ref[h])
            res = _dot(jnp.concatenate([w_ref[rows, hc], qg_ref[rows, hc]], axis=0), s.astype(BF16))
            vnb = (u_ref[rows, hc] - res[:CHUNK]).astype(BF16)
            s_new = s * dl_ref[CHUNK * j:CHUNK * j + 1, hc] + _dot_tn(kd_ref[rows, hc], vnb)
            s_ref[h] = s_new
            s_out_ref[seq, h] = s_new
            vns.append(vnb)
            outs.append(res[CHUNK:])
        vn_all = jnp.concatenate(vns, axis=1)
        o_all = jnp.concatenate(outs, axis=1) + _dot(qk_ref[j], block_diag(vn_all, DN_DV))
        for h in range(DN_HEADS):
            hc = slice(h * DN_DV, (h + 1) * DN_DV)
            o = o_all[:, hc]
            on = o * lax.rsqrt(jnp.mean(o * o, axis=-1, keepdims=True) + EPS) * onorm_ref[...]
            ob_ref[rows, hc] = (on * _silu(gate_ref[rows, hc].astype(F32))).astype(BF16)


def _chunk_selectors():
    r = jnp.arange(TILE)[:, None]
    c = jnp.arange(TILE)[None, :]
    same_chunk = (r // CHUNK) == (c // CHUNK)
    low = same_chunk & (c <= r)
    upp = same_chunk & (r <= c)
    last = c == (r // CHUNK) * CHUNK + (CHUNK - 1)
    return jnp.stack([low, upp, last]).astype(BF16)


def _delta(h, ab, abt, hist, s0, dw, alog_r, dtb_r, alog_c, dtb_c, onorm, n_prompt_chunks):
    t = h.shape[0]
    n_seq = hist.shape[0]
    full = lambda *shape: pl.BlockSpec(shape, lambda i: (0,) * len(shape))
    return pl.pallas_call(
        functools.partial(_delta_kernel, n_prompt_chunks=n_prompt_chunks),
        grid=(t // TILE,),
        in_specs=[
            pl.BlockSpec((TILE, DN_QKV), lambda i: (i, H_BQKV // DN_QKV)),
            pl.BlockSpec((TILE, 2 * LANES), lambda i: (i, 0)),
            pl.BlockSpec((2 * SUBLANES, TILE), lambda i: (0, i)),
            pl.BlockSpec((TILE, DN_HEADS * DN_DV), lambda i: (i, H_BG // (DN_HEADS * DN_DV))),
            full(n_seq, B_HIST, DN_QKV),
            full(n_seq, DN_HEADS, DN_DK, DN_DV),
            full(DN_CONV, DN_QKV),
            full(1, LANES), full(1, LANES), full(SUBLANES, 1), full(SUBLANES, 1),
            full(1, DN_DV),
            full(3, TILE, TILE),
        ],
        out_specs=[
            pl.BlockSpec((TILE, DN_HEADS * DN_DV), lambda i: (i, 0)),
            full(n_seq, B_HIST, DN_QKV),
            full(n_seq, DN_HEADS, DN_DK, DN_DV),
        ],
        out_shape=[
            jax.ShapeDtypeStruct((t, DN_HEADS * DN_DV), BF16),
            jax.ShapeDtypeStruct((n_seq, B_HIST, DN_QKV), F32),
            jax.ShapeDtypeStruct((n_seq, DN_HEADS, DN_DK, DN_DV), F32),
        ],
        scratch_shapes=[
            pltpu.VMEM((B_HIST + TILE, DN_QKV), F32),
            pltpu.VMEM((TILE, DN_QKV), F32),
            *[pltpu.VMEM((TILE, DN_QK), BF16) for _ in range(7)],
            pltpu.VMEM((TILE, DN_HEADS * CHUNK), F32),
            pltpu.VMEM((TILE, DN_QK), F32),
            pltpu.VMEM((TILE, DN_HEADS * DN_DV), F32),
            pltpu.VMEM((TILE, DN_QK), BF16),
            pltpu.VMEM((CPT, CHUNK, DN_HEADS * CHUNK), BF16),
            pltpu.VMEM((DN_HEADS, DN_DK, DN_DV), F32),
        ],
        compiler_params=_cparams(("arbitrary",)),
        name="delta",
    )(h, ab, abt, h, hist, s0, dw, alog_r, dtb_r, alog_c, dtb_c, onorm, _chunk_selectors())


PAIR = 2 * CHUNK
PAIR_BAND = (ATT_PREV + 2) * CHUNK


def _bias_table_kernel(rb_ref, out_ref):
    rel = lax.broadcasted_iota(jnp.int32, (2 * LANES, PAIR_BAND), 0)
    key = lax.broadcasted_iota(jnp.int32, (2 * LANES, PAIR_BAND), 1)
    key_chunk = lax.broadcasted_iota(jnp.int32, (ATT_HEADS, PAIR_BAND), 1) // CHUNK
    rb = rb_ref[...]
    for a in range(PAIR):
        idx = jnp.clip(ATT_PREV * CHUNK + a - key, REL_MIN, REL_MAX) - REL_MIN
        onehot = jnp.where(rel == idx, 1.0, 0.0).astype(BF16)
        band_pos = key_chunk - a // CHUNK
        seen = jnp.logical_and(band_pos >= 0, band_pos <= ATT_PREV)
        out_ref[:, a, :] = jnp.where(seen, _dot_sel_right(rb, onehot), NEG_INF)


def _bias_table(rel_bias):
    rb = jnp.pad(rel_bias, ((0, 0), (0, 2 * LANES - N_REL)))
    return pl.pallas_call(
        _bias_table_kernel,
        out_shape=jax.ShapeDtypeStruct((ATT_HEADS, PAIR, PAIR_BAND), F32),
        compiler_params=pltpu.CompilerParams(vmem_limit_bytes=VMEM_LIMIT),
        name="bias_table",
    )(rb)


def _attend(q, kband_ref, vband_ref, row0, n_keys, bias_of_head, first_valid_col):
    n_q = q.shape[0]
    lane = lax.broadcasted_iota(jnp.int32, (n_q, LANES), 1)
    valid = lax.broadcasted_iota(jnp.int32, (n_q, n_keys), 1) >= first_valid_col
    zero = jnp.zeros((), BF16)
    qs = q * jnp.asarray(ATT_HD ** -0.5, BF16)
    heads = range(ATT_HEADS)
    col_of = lambda h: slice((h // 2) * LANES, (h // 2 + 1) * LANES)
    s = []
    for h in heads:
        in_head = (lane < ATT_HD) if h % 2 == 0 else (lane >= ATT_HD)
        k2 = kband_ref[pl.ds(row0, n_keys), col_of(h)]
        s.append(_dot_nt(jnp.where(in_head, qs[:, col_of(h)], zero), k2))
    s = [jnp.where(valid, s[h] + bias_of_head(h), NEG_INF) for h in heads]
    top = [jnp.max(s[h], axis=-1, keepdims=True) for h in heads]
    p = [jnp.exp(s[h] - top[h]) for h in heads]
    denom = [jnp.sum(p[h], axis=-1, keepdims=True) for h in heads]
    o = [_dot(p[h].astype(BF16), vband_ref[pl.ds(row0, n_keys), col_of(h)]) for h in heads]
    o = [o[h] / denom[h] for h in heads]
    return [jnp.where(lane < ATT_HD, o[2 * hp], o[2 * hp + 1]) for hp in range(ATT_HEADS // 2)]


def _attn_prompt_kernel(q_ref, kp_ref, kc_ref, vp_ref, vc_ref, bias_ref, o_ref, kb_ref, vb_ref):
    i = pl.program_id(0)
    kb_ref[0:TILE, :] = kp_ref[...]
    kb_ref[TILE:2 * TILE, :] = kc_ref[...]
    vb_ref[0:TILE, :] = vp_ref[...]
    vb_ref[TILE:2 * TILE, :] = vc_ref[...]

    def pair_body(p, carry):
        base = pl.multiple_of(p * PAIR, PAIR)
        rows = pl.ds(base, PAIR)
        first_valid = jnp.maximum(ATT_PREV - (CPT * i + 2 * p), 0) * CHUNK
        outs = _attend(q_ref[rows, :], kb_ref, vb_ref, base, PAIR_BAND, lambda h: bias_ref[h], first_valid)
        for hp, o in enumerate(outs):
            o_ref[rows, hp * LANES:(hp + 1) * LANES] = o.astype(BF16)
        return carry

    lax.fori_loop(0, TILE // PAIR, pair_body, 0)


def _attn_prompt(h, bias, n_prompt_chunks):
    n_tiles = n_prompt_chunks // CPT
    qb, kb, vb = (H_C // ATT_W, H_C // ATT_W + 1, H_C // ATT_W + 2)
    prev = lambda i: jnp.maximum(i - 1, 0)
    return pl.pallas_call(
        _attn_prompt_kernel,
        grid=(n_tiles,),
        in_specs=[
            pl.BlockSpec((TILE, ATT_W), lambda i: (i, qb)),
            pl.BlockSpec((TILE, ATT_W), lambda i: (prev(i), kb)),
            pl.BlockSpec((TILE, ATT_W), lambda i: (i, kb)),
            pl.BlockSpec((TILE, ATT_W), lambda i: (prev(i), vb)),
            pl.BlockSpec((TILE, ATT_W), lambda i: (i, vb)),
            pl.BlockSpec((ATT_HEADS, PAIR, PAIR_BAND), lambda i: (0, 0, 0)),
        ],
        out_specs=pl.BlockSpec((TILE, ATT_W), lambda i: (i, 0)),
        out_shape=jax.ShapeDtypeStruct((n_tiles * TILE, ATT_W), BF16),
        scratch_shapes=[pltpu.VMEM((2 * TILE, ATT_W), BF16), pltpu.VMEM((2 * TILE, ATT_W), BF16)],
        compiler_params=_cparams(("arbitrary",)),
        name="attn_prompt",
    )(h, h, h, h, h, bias)


def _attn_sample_kernel(q_ref, k_ref, v_ref, ck_ref, cv_ref, bias_ref, o_ref, kb_ref, vb_ref):
    kb_ref[0:ATT_PREV * CHUNK, :] = ck_ref[0]
    kb_ref[ATT_PREV * CHUNK:BAND, :] = k_ref[...]
    vb_ref[0:ATT_PREV * CHUNK, :] = cv_ref[0]
    vb_ref[ATT_PREV * CHUNK:BAND, :] = v_ref[...]
    outs = _attend(q_ref[...], kb_ref, vb_ref, 0, BAND, lambda h: bias_ref[h, 0:CHUNK, 0:BAND], 0)
    for hp, o in enumerate(outs):
        o_ref[:, hp * LANES:(hp + 1) * LANES] = o.astype(BF16)


def _attn_sample(h, cache_k, cache_v, bias, n_prompt_chunks):
    n_seq = cache_k.shape[0]
    qb, kb, vb = (H_C // ATT_W, H_C // ATT_W + 1, H_C // ATT_W + 2)
    return pl.pallas_call(
        _attn_sample_kernel,
        grid=(n_seq,),
        in_specs=[
            pl.BlockSpec((CHUNK, ATT_W), lambda b: (n_prompt_chunks + b, qb)),
            pl.BlockSpec((CHUNK, ATT_W), lambda b: (n_prompt_chunks + b, kb)),
            pl.BlockSpec((CHUNK, ATT_W), lambda b: (n_prompt_chunks + b, vb)),
            pl.BlockSpec((1, ATT_PREV * CHUNK, ATT_W), lambda b: (b, 0, 0)),
            pl.BlockSpec((1, ATT_PREV * CHUNK, ATT_W), lambda b: (b, 0, 0)),
            pl.BlockSpec((ATT_HEADS, PAIR, PAIR_BAND), lambda b: (0, 0, 0)),
        ],
        out_specs=pl.BlockSpec((CHUNK, ATT_W), lambda b: (b, 0)),
        out_shape=jax.ShapeDtypeStruct((n_seq * CHUNK, ATT_W), BF16),
        scratch_shapes=[pltpu.VMEM((BAND, ATT_W), BF16), pltpu.VMEM((BAND, ATT_W), BF16)],
        compiler_params=_cparams(("arbitrary",)),
        name="attn_sample",
    )(h, h, h, cache_k, cache_v, bias)


def _merge_kernel(x_ref, a_ref, b_ref, c_ref, gate_ref, wa_ref, wb_ref, wc_ref, wo_ref, bg_ref, o_ref):
    merged = jnp.zeros((TILE, D_MODEL), F32)
    for n, (act_ref, w_ref) in enumerate(((a_ref, wa_ref), (b_ref, wb_ref), (c_ref, wc_ref))):
        y = _dot(act_ref[...], w_ref[...])
        z = gate_ref[:, n * D_MODEL:(n + 1) * D_MODEL].astype(F32) + bg_ref[n:n + 1, :]
        merged = merged + _sigmoid(z) * y
    o_ref[...] = x_ref[...] + _dot(merged.astype(BF16), wo_ref[...])


def _merge(x, act_a, act_b, act_c, h, wa, wb, wc, wo, b_gate):
    t = x.shape[0]
    full = lambda *shape: pl.BlockSpec(shape, lambda i: (0,) * len(shape))
    act = pl.BlockSpec((TILE, CONV_CH), lambda i: (i, 0))
    return pl.pallas_call(
        _merge_kernel,
        grid=(t // TILE,),
        in_specs=[
            pl.BlockSpec((TILE, D_MODEL), lambda i: (i, 0)),
            act, act, act,
            pl.BlockSpec((TILE, N_BRANCH * D_MODEL), lambda i: (i, H_GATE // (N_BRANCH * D_MODEL))),
            full(CONV_CH, D_MODEL), full(DN_HEADS * DN_DV, D_MODEL), full(ATT_W, D_MODEL),
            full(D_MODEL, D_MODEL), full(N_BRANCH, D_MODEL),
        ],
        out_specs=pl.BlockSpec((TILE, D_MODEL), lambda i: (i, 0)),
        out_shape=jax.ShapeDtypeStruct((t, D_MODEL), F32),
        compiler_params=_cparams(("parallel",)),
        name="merge",
    )(x, act_a, act_b, act_c, h, wa, wb, wc, wo, b_gate)


def _rms(x, g):
    return x * lax.rsqrt(jnp.mean(x * x, axis=-1, keepdims=True) + EPS) * g


def _ffn_kernel(x_ref, g_ref, w1_ref, w3_ref, w2_ref, fg_ref, o_ref, *, final_norm):
    x = x_ref[...]
    xn = _rms(x, g_ref[...]).astype(BF16)
    y = x
    for f in range(D_FF // FF_BLOCK):
        cols = slice(f * FF_BLOCK, (f + 1) * FF_BLOCK)
        hidden = _silu(_dot(xn, w1_ref[:, cols])) * _dot(xn, w3_ref[:, cols])
        y = y + _dot(hidden.astype(BF16), w2_ref[cols, :])
    o_ref[...] = _rms(y, fg_ref[...]) if final_norm else y


def _ffn(x, g, w1, w3, w2, final_g, final_norm):
    t = x.shape[0]
    once = pl.Buffered(1)
    return pl.pallas_call(
        functools.partial(_ffn_kernel, final_norm=final_norm),
        grid=(t // TILE,),
        in_specs=[
            pl.BlockSpec((TILE, D_MODEL), lambda i: (i, 0)),
            pl.BlockSpec((1, D_MODEL), lambda i: (0, 0)),
            pl.BlockSpec((D_MODEL, D_FF), lambda i: (0, 0), pipeline_mode=once),
            pl.BlockSpec((D_MODEL, D_FF), lambda i: (0, 0), pipeline_mode=once),
            pl.BlockSpec((D_FF, D_MODEL), lambda i: (0, 0), pipeline_mode=once),
            pl.BlockSpec((1, D_MODEL), lambda i: (0, 0)),
        ],
        out_specs=pl.BlockSpec((TILE, D_MODEL), lambda i: (i, 0)),
        out_shape=jax.ShapeDtypeStruct((t, D_MODEL), F32),
        compiler_params=_cparams(("parallel",)),
        name="ffn",
    )(x, g, w1, w3, w2, final_g)


MOE_TILE = 768
MOE_MAIN = 224
MOE_OVER = 128


def _moe_kernel(x_ref, g_ref, r_ref, before_ref, w1_ref, w3_ref, w2_ref, fg_ref, o_ref, *rest,
                final_norm, tail_rows):
    if tail_rows:
        tail_ref, *rest = rest
    xn_ref, gates_ref, key_ref, keyt_ref, cnt_ref, xc_ref, gs_ref, yc_ref = rest
    e = pl.program_id(1)
    f = pl.program_id(2)
    last_f = pl.num_programs(2) - 1
    lane = lax.broadcasted_iota(jnp.int32, (MOE_TILE, LANES), 1)

    @pl.when(jnp.logical_and(e == 0, f == 0))
    def _():
        x = x_ref[...]
        xn = _rms(x, g_ref[...])
        xn_ref[...] = xn.astype(BF16)
        o_ref[...] = x
        xh, xm, xl = _split3(xn)
        rh, rm, rl = _split3(r_ref[...])
        logits = (_dot(xh, rh) + (_dot(xh, rm) + _dot(xm, rh))
                  + (_dot(xh, rl) + _dot(xm, rm) + _dot(xl, rh)))
        logits = jnp.where(lane < N_EXPERTS, logits, -jnp.inf)
        m1 = jnp.max(logits, axis=-1, keepdims=True)
        i1 = jnp.min(jnp.where(logits == m1, lane, LANES), axis=-1, keepdims=True)
        others = jnp.where(lane == i1, -jnp.inf, logits)
        m2 = jnp.max(others, axis=-1, keepdims=True)
        i2 = jnp.min(jnp.where(others == m2, lane, LANES), axis=-1, keepdims=True)
        e2 = jnp.exp(m2 - m1)
        w_top = 1.0 / (1.0 + e2)
        gates_ref[...] = jnp.where(lane == i1, w_top, 0.0) + jnp.where(lane == i2, e2 * w_top, 0.0)
        routed = jnp.where(lane == i1, 1.0, jnp.where(lane == i2, 1.0, 0.0))
        key = jnp.where(routed > 0.0, _dot(before_ref[...], routed.astype(BF16)), -1.0)
        key_ref[...] = key
        keyt_ref[...] = key.T
        cnt_ref[...] = jnp.broadcast_to(jnp.sum(routed, axis=0, keepdims=True), (SUBLANES, LANES))

    lane1 = lax.broadcasted_iota(jnp.int32, (1, LANES), 1)
    n_routed = jnp.sum(jnp.where(lane1 == e, cnt_ref[0:1, :], 0.0)).astype(jnp.int32)
    n_over = jnp.maximum(n_routed - MOE_MAIN + (MOE_OVER - 1), 0) // MOE_OVER
    key_row = keyt_ref[pl.ds(e, 1), :]

    def block(r0, n_rows):
        rows = pl.ds(r0, n_rows)
        first = jnp.asarray(r0, jnp.int32).astype(F32)

        @pl.when(f == 0)
        def _():
            slot = lax.broadcasted_iota(jnp.int32, (n_rows, MOE_TILE), 0).astype(F32)
            pick = jnp.where(key_row - first == slot, 1.0, 0.0).astype(BF16)
            xc_ref[rows, :] = _dot(pick, xn_ref[...]).astype(BF16)
            gs_ref[rows, :] = _dot_sel_left(pick, gates_ref[...])

        xc = xc_ref[rows, :]
        lane_r = lax.broadcasted_iota(jnp.int32, (n_rows, LANES), 1)
        ge = jnp.sum(jnp.where(lane_r == e, gs_ref[rows, :], 0.0), axis=-1, keepdims=True)
        hidden = _silu(_dot(xc, w1_ref[0])) * _dot(xc, w3_ref[0])
        contrib = _dot((hidden * ge).astype(BF16), w2_ref[0])

        @pl.when(f == 0)
        def _():
            yc_ref[rows, :] = contrib

        @pl.when(f > 0)
        def _():
            yc_ref[rows, :] += contrib

        @pl.when(f == last_f)
        def _():
            key_col = jnp.sum(jnp.where(lane == e, key_ref[...], 0.0), axis=-1, keepdims=True)
            slot = lax.broadcasted_iota(jnp.int32, (MOE_TILE, n_rows), 1).astype(F32)
            place = jnp.where(key_col - first == slot, 1.0, 0.0).astype(BF16)
            o_ref[...] += _dot(place, yc_ref[rows, :].astype(BF16))

    block(0, MOE_MAIN)

    def over_body(b, carry):
        block(pl.multiple_of(MOE_MAIN + b * MOE_OVER, MOE_OVER), MOE_OVER)
        return carry

    lax.fori_loop(0, n_over, over_body, 0)

    last_step = jnp.logical_and(e == pl.num_programs(1) - 1, f == last_f)
    if final_norm:
        @pl.when(last_step)
        def _():
            o_ref[...] = _rms(o_ref[...], fg_ref[...])

    if tail_rows:
        @pl.when(jnp.logical_and(last_step, pl.program_id(0) == pl.num_programs(0) - 1))
        def _():
            tail_ref[...] = o_ref[MOE_TILE - tail_rows:, :]


def _moe(x, g, router, w1, w3, w2, final_g, final_norm, head_rows=None):
    t = x.shape[0]
    tail_rows = 0 if head_rows is None else t - head_rows
    assert tail_rows == 0 or (tail_rows <= MOE_TILE and tail_rows % SUBLANES == 0)
    r = jnp.arange(MOE_TILE)
    before = (r[None, :] < r[:, None]).astype(BF16)
    out_specs = [pl.BlockSpec((MOE_TILE, D_MODEL), lambda i, e, f: (i, 0))]
    out_shape = [jax.ShapeDtypeStruct((t - tail_rows, D_MODEL), F32)]
    if tail_rows:
        out_specs.append(pl.BlockSpec((tail_rows, D_MODEL), lambda i, e, f: (0, 0)))
        out_shape.append(jax.ShapeDtypeStruct((tail_rows, D_MODEL), F32))
    return pl.pallas_call(
        functools.partial(_moe_kernel, final_norm=final_norm, tail_rows=tail_rows),
        grid=(t // MOE_TILE, N_EXPERTS, D_FF // FF_BLOCK),
        in_specs=[
            pl.BlockSpec((MOE_TILE, D_MODEL), lambda i, e, f: (i, 0)),
            pl.BlockSpec((1, D_MODEL), lambda i, e, f: (0, 0)),
            pl.BlockSpec((D_MODEL, LANES), lambda i, e, f: (0, 0)),
            pl.BlockSpec((MOE_TILE, MOE_TILE), lambda i, e, f: (0, 0)),
            pl.BlockSpec((1, D_MODEL, FF_BLOCK), lambda i, e, f: (e, 0, f)),
            pl.BlockSpec((1, D_MODEL, FF_BLOCK), lambda i, e, f: (e, 0, f)),
            pl.BlockSpec((1, FF_BLOCK, D_MODEL), lambda i, e, f: (e, f, 0)),
            pl.BlockSpec((1, D_MODEL), lambda i, e, f: (0, 0)),
        ],
        out_specs=out_specs,
        out_shape=out_shape,
        scratch_shapes=[
            pltpu.VMEM((MOE_TILE, D_MODEL), BF16),
            pltpu.VMEM((MOE_TILE, LANES), F32),
            pltpu.VMEM((MOE_TILE, LANES), F32),
            pltpu.VMEM((LANES, MOE_TILE), F32),
            pltpu.VMEM((SUBLANES, LANES), F32),
            pltpu.VMEM((MOE_TILE, D_MODEL), BF16),
            pltpu.VMEM((MOE_TILE, LANES), F32),
            pltpu.VMEM((MOE_TILE, D_MODEL), F32),
        ],
        compiler_params=_cparams(("arbitrary", "arbitrary", "arbitrary")),
        name="moe",
    )(x, g, router, before, w1, w3, w2, final_g)


def _pack_w_in(w):
    main = jnp.concatenate([
        w[:, OFF_BQKV:OFF_BQKV + DN_QKV],
        w[:, OFF_C:OFF_C + 3 * ATT_W],
        w[:, OFF_GATE:OFF_GATE + N_BRANCH * D_MODEL],
        w[:, OFF_A:OFF_A + 2 * CONV_CH],
        w[:, OFF_BG:OFF_BG + DN_HEADS * DN_DV],
    ], axis=1).astype(BF16)
    wa = w[:, OFF_BA:OFF_BA + DN_HEADS]
    wb = w[:, OFF_BB:OFF_BB + DN_HEADS]
    zc = jnp.zeros((D_MODEL, LANES - DN_HEADS), w.dtype)
    wab = jnp.concatenate([wa, zc, wb, zc], axis=1).astype(BF16)
    zr = jnp.zeros((SUBLANES - DN_HEADS, D_MODEL), w.dtype)
    wabt = jnp.concatenate([wa.T, zr, wb.T, zr], axis=0).astype(BF16)
    return main, wab, wabt


def _lane_row(v):
    return jnp.pad(v.astype(F32), (0, LANES - v.shape[0]))[None, :]


def _sublane_col(v):
    return jnp.pad(v.astype(F32), (0, SUBLANES - v.shape[0]))[:, None]


def kernel(x_prompt, x_sample, cache_conv_a, state_sconv_b, state_delta_b, cache_k_c, cache_v_c, norm1_g, w_in, b_gate, dw_a, dwb_a, ln_a_g, ln_a_b, w_a_out, dw_b, a_log, dt_bias, onorm_b, w_b_out, rel_bias, w_c_out, w_out, norm2_g, ffn_w1, ffn_w3, ffn_w2, router, moe_w1, moe_w3, moe_w2, final_norm_g):
    bp, seq_len, _ = x_prompt.shape
    n_samp, samp_len, _ = x_sample.shape
    depth = w_in.shape[0]
    assert bp == 1 and samp_len == CHUNK and seq_len % TILE == 0 and (n_samp * CHUNK) % TILE == 0
    assert (seq_len + n_samp * CHUNK) % MOE_TILE == 0
    n_prompt_chunks = seq_len // CHUNK
    n_prompt = seq_len
    keep = min(ATT_PREV * CHUNK, seq_len)
    dt = x_prompt.dtype

    x = jnp.concatenate([x_prompt.reshape(seq_len, D_MODEL), x_sample.reshape(n_samp * CHUNK, D_MODEL)], axis=0)
    fg = final_norm_g[None, :]
    states = []
    for l in range(depth):
        w_main, wab, wabt = _pack_w_in(w_in[l])
        h, ab, abt = _in_proj(x, norm1_g[l][None, :], w_main, wab, wabt)

        hist_a = jnp.pad(jnp.concatenate([jnp.zeros((1,) + cache_conv_a.shape[2:], dt), cache_conv_a[l]], axis=0),
                         ((0, 0), (A_HIST - (CONV_WIDTH - 1), 0), (0, 0)))
        act_a, st_a = _conv_a(h, hist_a, dw_a[l], dwb_a[l][None, :], ln_a_g[l][None, :], ln_a_b[l][None, :],
                              n_prompt_chunks)

        hist_b = jnp.pad(jnp.concatenate([jnp.zeros((1,) + state_sconv_b.shape[2:], dt), state_sconv_b[l]], axis=0),
                         ((0, 0), (B_HIST - (DN_CONV - 1), 0), (0, 0)))
        s0 = jnp.concatenate([jnp.zeros((1,) + state_delta_b.shape[2:], dt), state_delta_b[l]], axis=0)
        act_b, st_sc, st_s = _delta(h, ab, abt, hist_b, s0, dw_b[l], _lane_row(a_log[l]), _lane_row(dt_bias[l]),
                                    _sublane_col(a_log[l]), _sublane_col(dt_bias[l]), onorm_b[l][None, :],
                                    n_prompt_chunks)

        bias = _bias_table(rel_bias[l])
        ck = cache_k_c[l].reshape(n_samp, -1, ATT_W).astype(BF16)
        cv = cache_v_c[l].reshape(n_samp, -1, ATT_W).astype(BF16)
        act_c = jnp.concatenate([_attn_prompt(h, bias, n_prompt_chunks),
                                 _attn_sample(h, ck, cv, bias, n_prompt_chunks)], axis=0)

        x = _merge(x, act_a, act_b, act_c, h, w_a_out[l].astype(BF16), w_b_out[l].astype(BF16),
                   w_c_out[l].astype(BF16), w_out[l].astype(BF16), b_gate[l])

        last = l == depth - 1
        j = l // 2
        if l % 2 == 0:
            x = _ffn(x, norm2_g[l][None, :], ffn_w1[j].astype(BF16), ffn_w3[j].astype(BF16),
                     ffn_w2[j].astype(BF16), fg, last)
        else:
            outs = _moe(x, norm2_g[l][None, :], jnp.pad(router[j], ((0, 0), (0, LANES - N_EXPERTS))),
                        moe_w1[j].astype(BF16), moe_w3[j].astype(BF16), moe_w2[j].astype(BF16), fg, last,
                        head_rows=n_prompt if last else None)
            x = outs[0] if len(outs) == 1 else outs

        k_new = h[:, H_C + ATT_W:H_C + 2 * ATT_W].astype(dt)
        v_new = h[:, H_C + 2 * ATT_W:H_C + 3 * ATT_W].astype(dt)
        states.append(dict(
            conv=st_a[:, A_HIST - (CONV_WIDTH - 1):, :],
            sconv=st_sc[:, B_HIST - (DN_CONV - 1):, :],
            delta=st_s,
            k=k_new, v=v_new))

    def stack(fn):
        return jnp.stack([fn(s) for s in states])

    heads = (ATT_HEADS, ATT_HD)
    y_head, y_tail = x if isinstance(x, (list, tuple)) else (x[:n_prompt], x[n_prompt:])
    y_prompt = y_head.reshape(1, seq_len, D_MODEL)
    y_sample = y_tail.reshape(n_samp, CHUNK, D_MODEL)
    return (
        y_prompt, y_sample,
        stack(lambda s: s["conv"][:1]), stack(lambda s: s["sconv"][:1]), stack(lambda s: s["delta"][:1]),
        stack(lambda s: s["k"][n_prompt - keep:n_prompt].reshape(1, keep, *heads)),
        stack(lambda s: s["v"][n_prompt - keep:n_prompt].reshape(1, keep, *heads)),
        stack(lambda s: s["conv"][1:]), stack(lambda s: s["sconv"][1:]), stack(lambda s: s["delta"][1:]),
        stack(lambda s: s["k"][n_prompt:].reshape(n_samp, CHUNK, *heads)),
        stack(lambda s: s["v"][n_prompt:].reshape(n_samp, CHUNK, *heads)),
    )
```

```python
import functools

import jax
import jax.numpy as jnp
from jax import lax
from jax.experimental import pallas as pl
from jax.experimental.pallas import tpu as pltpu

F32 = jnp.float32
BF16 = jnp.bfloat16

D_MODEL = 1024
CHUNK = 64
CONV_CH = 512
CONV_WIDTH = 31
DN_HEADS = 4
DN_DK = 128
DN_DV = 128
DN_CONV = 4
DN_QK = DN_HEADS * DN_DK
DN_QKV = DN_HEADS * (2 * DN_DK + DN_DV)
ATT_HEADS = 8
ATT_HD = 64
ATT_W = ATT_HEADS * ATT_HD
ATT_PREV = 8
BAND = (ATT_PREV + 1) * CHUNK
REL_MAX = 128
REL_MIN = -(CHUNK - 1)
N_REL = REL_MAX - REL_MIN + 1
D_FF = 2816
N_EXPERTS = 8
N_BRANCH = 3
EPS = 1e-6
NEG_INF = -1e30

OFF_A = 0
OFF_BQKV = OFF_A + 2 * CONV_CH
OFF_BA = OFF_BQKV + DN_QKV
OFF_BB = OFF_BA + DN_HEADS
OFF_BG = OFF_BB + DN_HEADS
OFF_C = OFF_BG + DN_HEADS * DN_DV
OFF_GATE = OFF_C + 3 * ATT_W

H_BQKV = 0
H_C = 1536
H_GATE = 3072
H_A = 6144
H_BG = 7168
H_COLS = 7680
H_BLOCK = 1536

LANES = 128
SUBLANES = 8
TILE = 512
CPT = TILE // CHUNK
FF_BLOCK = D_FF // 2
FF_CHUNK = 256
VMEM_LIMIT = 48 * 1024 * 1024


def _cparams(sem):
    return pltpu.CompilerParams(dimension_semantics=sem, vmem_limit_bytes=VMEM_LIMIT)


def _split3(x):
    hi = x.astype(BF16)
    r1 = x - hi.astype(F32)
    mid = r1.astype(BF16)
    lo = (r1 - mid.astype(F32)).astype(BF16)
    return hi, mid, lo


def _dot(a, b):
    return jnp.dot(a, b, preferred_element_type=F32)


def _dot_nt(a, b):
    return lax.dot_general(a, b, (((1,), (1,)), ((), ())), preferred_element_type=F32)


def _dot_tn(a, b):
    return lax.dot_general(a, b, (((0,), (0,)), ((), ())), preferred_element_type=F32)


def _dot_sel_left(sel_bf16, x):
    hi, mid, lo = _split3(x)
    return _dot(sel_bf16, hi) + _dot(sel_bf16, mid) + _dot(sel_bf16, lo)


def _dot_sel_right(x, sel_bf16):
    hi, mid, lo = _split3(x)
    return _dot(hi, sel_bf16) + _dot(mid, sel_bf16) + _dot(lo, sel_bf16)


def _sigmoid(x):
    return 1.0 / (1.0 + jnp.exp(-x))


def _silu(x):
    return x * _sigmoid(x)


def _in_proj_kernel(x_ref, g_ref, w_ref, wab_ref, wabt_ref, h_ref, ab_ref, abt_ref):
    x = x_ref[...]
    xnb = (x * lax.rsqrt(jnp.mean(x * x, axis=-1, keepdims=True) + EPS) * g_ref[...]).astype(BF16)
    ab_ref[...] = _dot(xnb, wab_ref[...])
    abt_ref[...] = _dot_nt(wabt_ref[...], xnb)
    for j in range(H_COLS // H_BLOCK):
        cols = slice(j * H_BLOCK, (j + 1) * H_BLOCK)
        h_ref[:, cols] = _dot(xnb, w_ref[:, cols]).astype(BF16)


def _in_proj(x, g, w, wab, wabt):
    t = x.shape[0]
    once = pl.Buffered(1)
    return pl.pallas_call(
        _in_proj_kernel,
        grid=(t // TILE,),
        in_specs=[
            pl.BlockSpec((TILE, D_MODEL), lambda i: (i, 0)),
            pl.BlockSpec((1, D_MODEL), lambda i: (0, 0)),
            pl.BlockSpec((D_MODEL, H_COLS), lambda i: (0, 0), pipeline_mode=once),
            pl.BlockSpec((D_MODEL, 2 * LANES), lambda i: (0, 0)),
            pl.BlockSpec((2 * SUBLANES, D_MODEL), lambda i: (0, 0)),
        ],
        out_specs=[
            pl.BlockSpec((TILE, H_COLS), lambda i: (i, 0)),
            pl.BlockSpec((TILE, 2 * LANES), lambda i: (i, 0)),
            pl.BlockSpec((2 * SUBLANES, TILE), lambda i: (0, i)),
        ],
        out_shape=[
            jax.ShapeDtypeStruct((t, H_COLS), BF16),
            jax.ShapeDtypeStruct((t, 2 * LANES), F32),
            jax.ShapeDtypeStruct((2 * SUBLANES, t), F32),
        ],
        compiler_params=_cparams(("parallel",)),
        name="in_proj",
    )(x, g, w, wab, wabt)


A_HIST = 32


def _chunk_flags(c, n_prompt_chunks):
    is_start = jnp.logical_or(c == 0, c >= n_prompt_chunks)
    is_end = c >= n_prompt_chunks - 1
    seq = jnp.maximum(c - (n_prompt_chunks - 1), 0)
    return is_start, is_end, seq


def _conv_a_kernel(h_ref, hist_ref, dw_ref, dwb_ref, lng_ref, lnb_ref, act_ref, st_ref,
                   xp_ref, cv_ref, *, n_prompt_chunks):
    i = pl.program_id(0)
    hv = h_ref[...]
    xp_ref[A_HIST:A_HIST + TILE, :] = (
        hv[:, :CONV_CH].astype(F32) * _sigmoid(hv[:, CONV_CH:].astype(F32)))
    first_tap = A_HIST - (CONV_WIDTH - 1)
    for j in range(CPT):
        base = CHUNK * j
        is_start, is_end, seq = _chunk_flags(CPT * i + j, n_prompt_chunks)

        @pl.when(is_start)
        def _():
            xp_ref[base:base + A_HIST, :] = hist_ref[seq]

        for grp in range(CONV_CH // LANES):
            cols = slice(grp * LANES, (grp + 1) * LANES)
            y = None
            for b in range(SUBLANES):
                n_rows = CHUNK if b == 0 else CHUNK + SUBLANES
                z = None
                for a in range((first_tap + CONV_WIDTH - 1) // SUBLANES + 1):
                    k = SUBLANES * a + b - first_tap
                    if 0 <= k < CONV_WIDTH:
                        row0 = base + SUBLANES * a
                        term = xp_ref[row0:row0 + n_rows, cols] * dw_ref[k:k + 1, cols]
                        z = term if z is None else z + term
                zb = z[b:b + CHUNK]
                y = zb if y is None else y + zb
            cv_ref[base:base + CHUNK, cols] = y

        @pl.when(is_end)
        def _():
            st_ref[seq] = xp_ref[base + CHUNK:base + CHUNK + A_HIST, :]

    xp_ref[0:A_HIST, :] = xp_ref[TILE:TILE + A_HIST, :]
    y = cv_ref[...] + dwb_ref[...]
    yc = y - jnp.mean(y, axis=-1, keepdims=True)
    yn = yc * lax.rsqrt(jnp.mean(yc * yc, axis=-1, keepdims=True) + EPS)
    act_ref[...] = _silu(yn * lng_ref[...] + lnb_ref[...]).astype(BF16)


def _conv_a(h, hist, dw, dwb, lng, lnb, n_prompt_chunks):
    t = h.shape[0]
    n_seq = hist.shape[0]
    full = lambda *shape: pl.BlockSpec(shape, lambda i: (0,) * len(shape))
    return pl.pallas_call(
        functools.partial(_conv_a_kernel, n_prompt_chunks=n_prompt_chunks),
        grid=(t // TILE,),
        in_specs=[
            pl.BlockSpec((TILE, 2 * CONV_CH), lambda i: (i, H_A // (2 * CONV_CH))),
            full(n_seq, A_HIST, CONV_CH),
            full(CONV_WIDTH, CONV_CH),
            full(1, CONV_CH), full(1, CONV_CH), full(1, CONV_CH),
        ],
        out_specs=[
            pl.BlockSpec((TILE, CONV_CH), lambda i: (i, 0)),
            full(n_seq, A_HIST, CONV_CH),
        ],
        out_shape=[
            jax.ShapeDtypeStruct((t, CONV_CH), BF16),
            jax.ShapeDtypeStruct((n_seq, A_HIST, CONV_CH), F32),
        ],
        scratch_shapes=[
            pltpu.VMEM((A_HIST + TILE, CONV_CH), F32),
            pltpu.VMEM((TILE, CONV_CH), F32),
        ],
        compiler_params=_cparams(("arbitrary",)),
        name="conv_a",
    )(h, hist, dw, dwb, lng, lnb)


B_HIST = SUBLANES


def _delta_kernel(qkv_ref, ab_ref, abt_ref, gate_ref, hist_ref, s0_ref, dw_ref,
                  alog_r_ref, dtb_r_ref, alog_c_ref, dtb_c_ref, onorm_ref, sel_ref,
                  ob_ref, sc_out_ref, s_out_ref,
                  xq_ref, pc_ref, qb_ref, kb_ref, kbeta_ref, vb_ref, kbe_ref, qg_ref, kd_ref,
                  gcx_ref, dl_ref, u_ref, w_ref, qk_ref, s_ref, *, n_prompt_chunks):
    i = pl.program_id(0)
    first_tap = B_HIST - (DN_CONV - 1)

    @pl.when(i == 0)
    def _():
        s_ref[...] = jnp.zeros(s_ref.shape, F32)

    xq_ref[B_HIST:B_HIST + TILE, :] = qkv_ref[...].astype(F32)
    for j in range(CPT):
        base = CHUNK * j
        is_start, is_end, seq = _chunk_flags(CPT * i + j, n_prompt_chunks)

        @pl.when(is_start)
        def _():
            xq_ref[base:base + B_HIST, :] = hist_ref[seq]

        for grp in range(DN_QKV // LANES):
            cols = slice(grp * LANES, (grp + 1) * LANES)
            acc = jnp.zeros((CHUNK, LANES), F32)
            for k in range(DN_CONV):
                off = base + first_tap + k
                acc = acc + xq_ref[off:off + CHUNK, cols] * dw_ref[k:k + 1, cols]
            pc_ref[base:base + CHUNK, cols] = acc

        @pl.when(is_end)
        def _():
            sc_out_ref[seq] = xq_ref[base + CHUNK:base + CHUNK + B_HIST, :]

    xq_ref[0:B_HIST, :] = xq_ref[TILE:TILE + B_HIST, :]

    for grp in range(DN_QKV // LANES):
        cols = slice(grp * LANES, (grp + 1) * LANES)
        x = _silu(pc_ref[:, cols])
        if grp < 2 * DN_HEADS:
            x = x * lax.rsqrt(jnp.sum(x * x, axis=-1, keepdims=True) + EPS)
        if grp < DN_HEADS:
            x = x * (DN_DK ** -0.5)
        pc_ref[:, cols] = x

    def log_decay(alpha, alog, dtb):
        z = alpha + dtb
        softplus = jnp.maximum(z, 0.0) + jnp.log(1.0 + jnp.exp(-jnp.abs(z)))
        return -jnp.exp(alog) * softplus

    ab = ab_ref[...]
    g_col = log_decay(ab[:, :LANES], alog_r_ref[...], dtb_r_ref[...])
    beta = _sigmoid(ab[:, LANES:])
    abt = abt_ref[...]
    g_row = log_decay(abt[:SUBLANES, :], alog_c_ref[...], dtb_c_ref[...])
    gcc = _dot_sel_left(sel_ref[0], g_col)
    gc_row = _dot_sel_right(g_row, sel_ref[1])
    gcl = _dot_sel_left(sel_ref[2], gcc)

    def expansion(width):
        hr = lax.broadcasted_iota(jnp.int32, (LANES, DN_HEADS * width), 0)
        hc = lax.broadcasted_iota(jnp.int32, (LANES, DN_HEADS * width), 1)
        return jnp.where(hc // width == hr, 1.0, 0.0).astype(BF16)

    wide = expansion(DN_DK)
    beta_x = _dot_sel_right(beta, wide)
    eg_x = _dot_sel_right(jnp.exp(gcc), wide)
    ekd_x = _dot_sel_right(jnp.exp(gcl - gcc), wide)
    dl_ref[...] = _dot_sel_right(jnp.exp(gcl), wide)
    gcx_ref[...] = _dot_sel_right(gcc, expansion(CHUNK))

    q = pc_ref[:, 0:DN_QK]
    k = pc_ref[:, DN_QK:2 * DN_QK]
    kbeta = k * beta_x
    qb_ref[...] = q.astype(BF16)
    kb_ref[...] = k.astype(BF16)
    kbeta_ref[...] = kbeta.astype(BF16)
    vb_ref[...] = (pc_ref[:, 2 * DN_QK:] * beta_x).astype(BF16)
    kbe_ref[...] = (kbeta * eg_x).astype(BF16)
    qg_ref[...] = (q * eg_x).astype(BF16)
    kd_ref[...] = (k * ekd_x).astype(BF16)

    def block_diag(y, width):
        blk = lax.broadcasted_iota(jnp.int32, y.shape, 1) // width
        zero = jnp.zeros((), y.dtype)
        return jnp.concatenate([jnp.where(blk == h, y, zero) for h in range(DN_HEADS)], axis=0)

    row4 = lax.broadcasted_iota(jnp.int32, (CHUNK, DN_HEADS * CHUNK), 0)
    col4 = lax.broadcasted_iota(jnp.int32, (CHUNK, DN_HEADS * CHUNK), 1) % CHUNK
    eye4 = jnp.where(row4 == col4, 1.0, 0.0).astype(F32)

    chunk_rows = [slice(CHUNK * j, CHUNK * (j + 1)) for j in range(CPT)]
    n_pow, t_inv = [None] * CPT, [None] * CPT
    for j, rows in enumerate(chunk_rows):
        lhs = jnp.concatenate([kbeta_ref[rows, :], qb_ref[rows, :]], axis=0)
        both = _dot_nt(lhs, block_diag(kb_ref[rows, :], DN_DK))
        gc_r = jnp.concatenate([gc_row[h:h + 1, rows] for h in range(DN_HEADS)], axis=1)
        gam = jnp.exp(jnp.where(row4 >= col4, gcx_ref[rows, :] - gc_r, -jnp.inf))
        qk_ref[j] = (both[CHUNK:] * gam).astype(BF16)
        n_pow[j] = jnp.where(row4 > col4, -(both[:CHUNK] * gam), 0.0)
        t_inv[j] = eye4 + n_pow[j]
    for j in range(CPT):
        nb = n_pow[j].astype(BF16)
        n_pow[j] = _dot(nb, block_diag(nb, CHUNK))
    for _ in range(4):
        for j in range(CPT):
            nb = n_pow[j].astype(BF16)
            res = _dot(jnp.concatenate([nb, t_inv[j].astype(BF16)], axis=0), block_diag(nb, CHUNK))
            n_pow[j] = res[:CHUNK]
            t_inv[j] = t_inv[j] + res[CHUNK:]
    for j, rows in enumerate(chunk_rows):
        t_fin = t_inv[j] + _dot(t_inv[j].astype(BF16), block_diag(n_pow[j].astype(BF16), CHUNK))
        tb = t_fin.astype(BF16)
        u_ref[rows, :] = _dot(tb, block_diag(vb_ref[rows, :], DN_DV))
        w_ref[rows, :] = _dot(tb, block_diag(kbe_ref[rows, :], DN_DK)).astype(BF16)

    for j, rows in enumerate(chunk_rows):
        is_start, _, seq = _chunk_flags(CPT * i + j, n_prompt_chunks)
        vns, outs = [], []
        for h in range(DN_HEADS):
            hc = slice(h * DN_DK, (h + 1) * DN_DK)
            s = jnp.where(is_start, s0_ref[seq, h], s_ref[h])
            res = _dot(jnp.concatenate([w_ref[rows, hc], qg_ref[rows, hc]], axis=0), s.astype(BF16))
            vnb = (u_ref[rows, hc] - res[:CHUNK]).astype(BF16)
            s_new = s * dl_ref[CHUNK * j:CHUNK * j + 1, hc] + _dot_tn(kd_ref[rows, hc], vnb)
            s_ref[h] = s_new
            s_out_ref[seq, h] = s_new
            vns.append(vnb)
            outs.append(res[CHUNK:])
        vn_all = jnp.concatenate(vns, axis=1)
        o_all = jnp.concatenate(outs, axis=1) + _dot(qk_ref[j], block_diag(vn_all, DN_DV))
        for h in range(DN_HEADS):
            hc = slice(h * DN_DV, (h + 1) * DN_DV)
            o = o_all[:, hc]
            on = o * lax.rsqrt(jnp.mean(o * o, axis=-1, keepdims=True) + EPS) * onorm_ref[...]
            ob_ref[rows, hc] = (on * _silu(gate_ref[rows, hc].astype(F32))).astype(BF16)


def _chunk_selectors():
    r = jnp.arange(TILE)[:, None]
    c = jnp.arange(TILE)[None, :]
    same_chunk = (r // CHUNK) == (c // CHUNK)
    low = same_chunk & (c <= r)
    upp = same_chunk & (r <= c)
    last = c == (r // CHUNK) * CHUNK + (CHUNK - 1)
    return jnp.stack([low, upp, last]).astype(BF16)


def _delta(h, ab, abt, hist, s0, dw, alog_r, dtb_r, alog_c, dtb_c, onorm, n_prompt_chunks):
    t = h.shape[0]
    n_seq = hist.shape[0]
    full = lambda *shape: pl.BlockSpec(shape, lambda i: (0,) * len(shape))
    return pl.pallas_call(
        functools.partial(_delta_kernel, n_prompt_chunks=n_prompt_chunks),
        grid=(t // TILE,),
        in_specs=[
            pl.BlockSpec((TILE, DN_QKV), lambda i: (i, H_BQKV // DN_QKV)),
            pl.BlockSpec((TILE, 2 * LANES), lambda i: (i, 0)),
            pl.BlockSpec((2 * SUBLANES, TILE), lambda i: (0, i)),
            pl.BlockSpec((TILE, DN_HEADS * DN_DV), lambda i: (i, H_BG // (DN_HEADS * DN_DV))),
            full(n_seq, B_HIST, DN_QKV),
            full(n_seq, DN_HEADS, DN_DK, DN_DV),
            full(DN_CONV, DN_QKV),
            full(1, LANES), full(1, LANES), full(SUBLANES, 1), full(SUBLANES, 1),
            full(1, DN_DV),
            full(3, TILE, TILE),
        ],
        out_specs=[
            pl.BlockSpec((TILE, DN_HEADS * DN_DV), lambda i: (i, 0)),
            full(n_seq, B_HIST, DN_QKV),
            full(n_seq, DN_HEADS, DN_DK, DN_DV),
        ],
        out_shape=[
            jax.ShapeDtypeStruct((t, DN_HEADS * DN_DV), BF16),
            jax.ShapeDtypeStruct((n_seq, B_HIST, DN_QKV), F32),
            jax.ShapeDtypeStruct((n_seq, DN_HEADS, DN_DK, DN_DV), F32),
        ],
        scratch_shapes=[
            pltpu.VMEM((B_HIST + TILE, DN_QKV), F32),
            pltpu.VMEM((TILE, DN_QKV), F32),
            *[pltpu.VMEM((TILE, DN_QK), BF16) for _ in range(7)],
            pltpu.VMEM((TILE, DN_HEADS * CHUNK), F32),
            pltpu.VMEM((TILE, DN_QK), F32),
            pltpu.VMEM((TILE, DN_HEADS * DN_DV), F32),
            pltpu.VMEM((TILE, DN_QK), BF16),
            pltpu.VMEM((CPT, CHUNK, DN_HEADS * CHUNK), BF16),
            pltpu.VMEM((DN_HEADS, DN_DK, DN_DV), F32),
        ],
        compiler_params=_cparams(("arbitrary",)),
        name="delta",
    )(h, ab, abt, h, hist, s0, dw, alog_r, dtb_r, alog_c, dtb_c, onorm, _chunk_selectors())


PAIR = 2 * CHUNK
PAIR_BAND = (ATT_PREV + 2) * CHUNK


def _bias_table_kernel(rb_ref, out_ref):
    rel = lax.broadcasted_iota(jnp.int32, (2 * LANES, PAIR_BAND), 0)
    key = lax.broadcasted_iota(jnp.int32, (2 * LANES, PAIR_BAND), 1)
    key_chunk = lax.broadcasted_iota(jnp.int32, (ATT_HEADS, PAIR_BAND), 1) // CHUNK
    rb = rb_ref[...]
    for a in range(PAIR):
        idx = jnp.clip(ATT_PREV * CHUNK + a - key, REL_MIN, REL_MAX) - REL_MIN
        onehot = jnp.where(rel == idx, 1.0, 0.0).astype(BF16)
        band_pos = key_chunk - a // CHUNK
        seen = jnp.logical_and(band_pos >= 0, band_pos <= ATT_PREV)
        out_ref[:, a, :] = jnp.where(seen, _dot_sel_right(rb, onehot), NEG_INF)


def _bias_table(rel_bias):
    rb = jnp.pad(rel_bias, ((0, 0), (0, 2 * LANES - N_REL)))
    return pl.pallas_call(
        _bias_table_kernel,
        out_shape=jax.ShapeDtypeStruct((ATT_HEADS, PAIR, PAIR_BAND), F32),
        compiler_params=pltpu.CompilerParams(vmem_limit_bytes=VMEM_LIMIT),
        name="bias_table",
    )(rb)


def _attend(q, kband_ref, vband_ref, row0, n_keys, bias_of_head, first_valid_col):
    n_q = q.shape[0]
    lane = lax.broadcasted_iota(jnp.int32, (n_q, LANES), 1)
    valid = lax.broadcasted_iota(jnp.int32, (n_q, n_keys), 1) >= first_valid_col
    zero = jnp.zeros((), BF16)
    qs = q * jnp.asarray(ATT_HD ** -0.5, BF16)
    heads = range(ATT_HEADS)
    col_of = lambda h: slice((h // 2) * LANES, (h // 2 + 1) * LANES)
    s = []
    for h in heads:
        in_head = (lane < ATT_HD) if h % 2 == 0 else (lane >= ATT_HD)
        k2 = kband_ref[pl.ds(row0, n_keys), col_of(h)]
        s.append(_dot_nt(jnp.where(in_head, qs[:, col_of(h)], zero), k2))
    s = [jnp.where(valid, s[h] + bias_of_head(h), NEG_INF) for h in heads]
    top = [jnp.max(s[h], axis=-1, keepdims=True) for h in heads]
    p = [jnp.exp(s[h] - top[h]) for h in heads]
    denom = [jnp.sum(p[h], axis=-1, keepdims=True) for h in heads]
    o = [_dot(p[h].astype(BF16), vband_ref[pl.ds(row0, n_keys), col_of(h)]) for h in heads]
    o = [o[h] / denom[h] for h in heads]
    return [jnp.where(lane < ATT_HD, o[2 * hp], o[2 * hp + 1]) for hp in range(ATT_HEADS // 2)]


def _attn_prompt_kernel(q_ref, kp_ref, kc_ref, vp_ref, vc_ref, bias_ref, o_ref, kb_ref, vb_ref):
    i = pl.program_id(0)
    kb_ref[0:TILE, :] = kp_ref[...]
    kb_ref[TILE:2 * TILE, :] = kc_ref[...]
    vb_ref[0:TILE, :] = vp_ref[...]
    vb_ref[TILE:2 * TILE, :] = vc_ref[...]

    def pair_body(p, carry):
        base = pl.multiple_of(p * PAIR, PAIR)
        rows = pl.ds(base, PAIR)
        first_valid = jnp.maximum(ATT_PREV - (CPT * i + 2 * p), 0) * CHUNK
        outs = _attend(q_ref[rows, :], kb_ref, vb_ref, base, PAIR_BAND, lambda h: bias_ref[h], first_valid)
        for hp, o in enumerate(outs):
            o_ref[rows, hp * LANES:(hp + 1) * LANES] = o.astype(BF16)
        return carry

    lax.fori_loop(0, TILE // PAIR, pair_body, 0)


def _attn_prompt(h, bias, n_prompt_chunks):
    n_tiles = n_prompt_chunks // CPT
    qb, kb, vb = (H_C // ATT_W, H_C // ATT_W + 1, H_C // ATT_W + 2)
    prev = lambda i: jnp.maximum(i - 1, 0)
    return pl.pallas_call(
        _attn_prompt_kernel,
        grid=(n_tiles,),
        in_specs=[
            pl.BlockSpec((TILE, ATT_W), lambda i: (i, qb)),
            pl.BlockSpec((TILE, ATT_W), lambda i: (prev(i), kb)),
            pl.BlockSpec((TILE, ATT_W), lambda i: (i, kb)),
            pl.BlockSpec((TILE, ATT_W), lambda i: (prev(i), vb)),
            pl.BlockSpec((TILE, ATT_W), lambda i: (i, vb)),
            pl.BlockSpec((ATT_HEADS, PAIR, PAIR_BAND), lambda i: (0, 0, 0)),
        ],
        out_specs=pl.BlockSpec((TILE, ATT_W), lambda i: (i, 0)),
        out_shape=jax.ShapeDtypeStruct((n_tiles * TILE, ATT_W), BF16),
        scratch_shapes=[pltpu.VMEM((2 * TILE, ATT_W), BF16), pltpu.VMEM((2 * TILE, ATT_W), BF16)],
        compiler_params=_cparams(("arbitrary",)),
        name="attn_prompt",
    )(h, h, h, h, h, bias)


def _attn_sample_kernel(q_ref, k_ref, v_ref, ck_ref, cv_ref, bias_ref, o_ref, kb_ref, vb_ref):
    kb_ref[0:ATT_PREV * CHUNK, :] = ck_ref[0]
    kb_ref[ATT_PREV * CHUNK:BAND, :] = k_ref[...]
    vb_ref[0:ATT_PREV * CHUNK, :] = cv_ref[0]
    vb_ref[ATT_PREV * CHUNK:BAND, :] = v_ref[...]
    outs = _attend(q_ref[...], kb_ref, vb_ref, 0, BAND, lambda h: bias_ref[h, 0:CHUNK, 0:BAND], 0)
    for hp, o in enumerate(outs):
        o_ref[:, hp * LANES:(hp + 1) * LANES] = o.astype(BF16)


def _attn_sample(h, cache_k, cache_v, bias, n_prompt_chunks):
    n_seq = cache_k.shape[0]
    qb, kb, vb = (H_C // ATT_W, H_C // ATT_W + 1, H_C // ATT_W + 2)
    return pl.pallas_call(
        _attn_sample_kernel,
        grid=(n_seq,),
        in_specs=[
            pl.BlockSpec((CHUNK, ATT_W), lambda b: (n_prompt_chunks + b, qb)),
            pl.BlockSpec((CHUNK, ATT_W), lambda b: (n_prompt_chunks + b, kb)),
            pl.BlockSpec((CHUNK, ATT_W), lambda b: (n_prompt_chunks + b, vb)),
            pl.BlockSpec((1, ATT_PREV * CHUNK, ATT_W), lambda b: (b, 0, 0)),
            pl.BlockSpec((1, ATT_PREV * CHUNK, ATT_W), lambda b: (b, 0, 0)),
            pl.BlockSpec((ATT_HEADS, PAIR, PAIR_BAND), lambda b: (0, 0, 0)),
        ],
        out_specs=pl.BlockSpec((CHUNK, ATT_W), lambda b: (b, 0)),
        out_shape=jax.ShapeDtypeStruct((n_seq * CHUNK, ATT_W), BF16),
        scratch_shapes=[pltpu.VMEM((BAND, ATT_W), BF16), pltpu.VMEM((BAND, ATT_W), BF16)],
        compiler_params=_cparams(("arbitrary",)),
        name="attn_sample",
    )(h, h, h, cache_k, cache_v, bias)


def _merge_kernel(x_ref, a_ref, b_ref, c_ref, gate_ref, wa_ref, wb_ref, wc_ref, wo_ref, bg_ref, o_ref):
    merged = jnp.zeros((TILE, D_MODEL), F32)
    for n, (act_ref, w_ref) in enumerate(((a_ref, wa_ref), (b_ref, wb_ref), (c_ref, wc_ref))):
        y = _dot(act_ref[...], w_ref[...])
        z = gate_ref[:, n * D_MODEL:(n + 1) * D_MODEL].astype(F32) + bg_ref[n:n + 1, :]
        merged = merged + _sigmoid(z) * y
    o_ref[...] = x_ref[...] + _dot(merged.astype(BF16), wo_ref[...])


def _merge(x, act_a, act_b, act_c, h, wa, wb, wc, wo, b_gate):
    t = x.shape[0]
    full = lambda *shape: pl.BlockSpec(shape, lambda i: (0,) * len(shape))
    act = pl.BlockSpec((TILE, CONV_CH), lambda i: (i, 0))
    return pl.pallas_call(
        _merge_kernel,
        grid=(t // TILE,),
        in_specs=[
            pl.BlockSpec((TILE, D_MODEL), lambda i: (i, 0)),
            act, act, act,
            pl.BlockSpec((TILE, N_BRANCH * D_MODEL), lambda i: (i, H_GATE // (N_BRANCH * D_MODEL))),
            full(CONV_CH, D_MODEL), full(DN_HEADS * DN_DV, D_MODEL), full(ATT_W, D_MODEL),
            full(D_MODEL, D_MODEL), full(N_BRANCH, D_MODEL),
        ],
        out_specs=pl.BlockSpec((TILE, D_MODEL), lambda i: (i, 0)),
        out_shape=jax.ShapeDtypeStruct((t, D_MODEL), F32),
        compiler_params=_cparams(("parallel",)),
        name="merge",
    )(x, act_a, act_b, act_c, h, wa, wb, wc, wo, b_gate)


def _rms(x, g):
    return x * lax.rsqrt(jnp.mean(x * x, axis=-1, keepdims=True) + EPS) * g


def _ffn_kernel(x_ref, g_ref, w1_ref, w3_ref, w2_ref, fg_ref, o_ref, *, final_norm):
    x = x_ref[...]
    xn = _rms(x, g_ref[...]).astype(BF16)
    y = x
    for f in range(D_FF // FF_CHUNK):
        cols = slice(f * FF_CHUNK, (f + 1) * FF_CHUNK)
        hidden = _silu(_dot(xn, w1_ref[:, cols])) * _dot(xn, w3_ref[:, cols])
        y = y + _dot(hidden.astype(BF16), w2_ref[cols, :])
    o_ref[...] = _rms(y, fg_ref[...]) if final_norm else y


def _ffn(x, g, w1, w3, w2, final_g, final_norm):
    t = x.shape[0]
    once = pl.Buffered(1)
    return pl.pallas_call(
        functools.partial(_ffn_kernel, final_norm=final_norm),
        grid=(t // TILE,),
        in_specs=[
            pl.BlockSpec((TILE, D_MODEL), lambda i: (i, 0)),
            pl.BlockSpec((1, D_MODEL), lambda i: (0, 0)),
            pl.BlockSpec((D_MODEL, D_FF), lambda i: (0, 0), pipeline_mode=once),
            pl.BlockSpec((D_MODEL, D_FF), lambda i: (0, 0), pipeline_mode=once),
            pl.BlockSpec((D_FF, D_MODEL), lambda i: (0, 0), pipeline_mode=once),
            pl.BlockSpec((1, D_MODEL), lambda i: (0, 0)),
        ],
        out_specs=pl.BlockSpec((TILE, D_MODEL), lambda i: (i, 0)),
        out_shape=jax.ShapeDtypeStruct((t, D_MODEL), F32),
        compiler_params=_cparams(("parallel",)),
        name="ffn",
    )(x, g, w1, w3, w2, final_g)


MOE_HALF = 768
MOE_HALVES = 2
MOE_TILE = MOE_HALF * MOE_HALVES
MOE_MAIN = 224
MOE_OVER = 128


def _moe_kernel(x_ref, g_ref, r_ref, before_ref, w1_ref, w3_ref, w2_ref, fg_ref, o_ref, *rest,
                final_norm, tail_rows):
    if tail_rows:
        tail_ref, *rest = rest
    xn_ref, gates_ref, key_ref, keyt_ref, cnt_ref, xc_ref, gs_ref, yc_ref = rest
    e = pl.program_id(1)
    f = pl.program_id(2)
    last_f = pl.num_programs(2) - 1
    lane = lax.broadcasted_iota(jnp.int32, (MOE_HALF, LANES), 1)
    half_rows = [slice(hh * MOE_HALF, (hh + 1) * MOE_HALF) for hh in range(MOE_HALVES)]
    main_rows = [slice(hh * MOE_MAIN, (hh + 1) * MOE_MAIN) for hh in range(MOE_HALVES)]

    @pl.when(jnp.logical_and(e == 0, f == 0))
    def _():
        o_ref[...] = x_ref[...]
        r3 = r_ref[...]
        for hh, rows in enumerate(half_rows):
            xn = _rms(x_ref[rows, :], g_ref[...])
            xn_ref[rows, :] = xn.astype(BF16)
            xh, xm, xl = _split3(xn)
            ph, pm, pl_ = _dot(xh, r3), _dot(xm, r3), _dot(xl, r3)
            down = lambda p, k: pltpu.roll(p, LANES - k * N_EXPERTS, axis=1)
            logits = (ph + (down(ph, 1) + pm)) + (down(ph, 2) + down(pm, 1) + pl_)
            logits = jnp.where(lane < N_EXPERTS, logits, -jnp.inf)
            m1 = jnp.max(logits, axis=-1, keepdims=True)
            i1 = jnp.min(jnp.where(logits == m1, lane, LANES), axis=-1, keepdims=True)
            others = jnp.where(lane == i1, -jnp.inf, logits)
            m2 = jnp.max(others, axis=-1, keepdims=True)
            i2 = jnp.min(jnp.where(others == m2, lane, LANES), axis=-1, keepdims=True)
            e2 = jnp.exp(m2 - m1)
            w_top = 1.0 / (1.0 + e2)
            gates_ref[hh] = jnp.where(lane == i1, w_top, 0.0) + jnp.where(lane == i2, e2 * w_top, 0.0)
            routed = jnp.where(lane == i1, 1.0, jnp.where(lane == i2, 1.0, 0.0))
            key = jnp.where(routed > 0.0, _dot(before_ref[...], routed.astype(BF16)), -1.0)
            key_ref[hh] = key
            keyt_ref[hh] = key.T
            cnt_ref[hh] = jnp.broadcast_to(jnp.sum(routed, axis=0, keepdims=True), (SUBLANES, LANES))

    def pick_of(hh, first, n_rows):
        slot = lax.broadcasted_iota(jnp.int32, (n_rows, MOE_HALF), 0).astype(F32)
        return jnp.where(keyt_ref[hh, pl.ds(e, 1), :] - first == slot, 1.0, 0.0).astype(BF16)

    def place_of(hh, first, n_rows):
        key_col = jnp.sum(jnp.where(lane == e, key_ref[hh], 0.0), axis=-1, keepdims=True)
        slot = lax.broadcasted_iota(jnp.int32, (MOE_HALF, n_rows), 1).astype(F32)
        return jnp.where(key_col - first == slot, 1.0, 0.0).astype(BF16)

    def expert(xc, gs):
        lane_r = lax.broadcasted_iota(jnp.int32, gs.shape, 1)
        ge = jnp.sum(jnp.where(lane_r == e, gs, 0.0), axis=-1, keepdims=True)
        hidden = _silu(_dot(xc, w1_ref[0])) * _dot(xc, w3_ref[0])
        return _dot((hidden * ge).astype(BF16), w2_ref[0])

    @pl.when(f == 0)
    def _():
        for hh in range(MOE_HALVES):
            pick = pick_of(hh, 0.0, MOE_MAIN)
            xc_ref[main_rows[hh], :] = _dot(pick, xn_ref[half_rows[hh], :]).astype(BF16)
            gs_ref[main_rows[hh], :] = _dot_sel_left(pick, gates_ref[hh])

    contrib = expert(xc_ref[...], gs_ref[...])

    assert D_FF // FF_BLOCK >= 2

    @pl.when(f == 0)
    def _():
        yc_ref[...] = contrib

    @pl.when(jnp.logical_and(f > 0, f < last_f))
    def _():
        yc_ref[...] += contrib

    @pl.when(f == last_f)
    def _():
        total = (yc_ref[...] + contrib).astype(BF16)
        for hh in range(MOE_HALVES):
            o_ref[half_rows[hh], :] += _dot(place_of(hh, 0.0, MOE_MAIN), total[main_rows[hh], :])

    lane1 = lax.broadcasted_iota(jnp.int32, (1, LANES), 1)
    for hh in range(MOE_HALVES):
        n_routed = jnp.sum(jnp.where(lane1 == e, cnt_ref[hh, 0:1, :], 0.0)).astype(jnp.int32)
        n_over = jnp.maximum(n_routed - MOE_MAIN + (MOE_OVER - 1), 0) // MOE_OVER

        def over_body(b, carry, hh=hh):
            first = (MOE_MAIN + b * MOE_OVER).astype(F32)
            pick = pick_of(hh, first, MOE_OVER)
            xo = _dot(pick, xn_ref[half_rows[hh], :]).astype(BF16)
            co = expert(xo, _dot_sel_left(pick, gates_ref[hh])).astype(BF16)
            o_ref[half_rows[hh], :] += _dot(place_of(hh, first, MOE_OVER), co)
            return carry

        lax.fori_loop(0, n_over, over_body, 0)

    last_step = jnp.logical_and(e == pl.num_programs(1) - 1, f == last_f)
    if final_norm:
        @pl.when(last_step)
        def _():
            o_ref[...] = _rms(o_ref[...], fg_ref[...])

    if tail_rows:
        @pl.when(jnp.logical_and(last_step, pl.program_id(0) == pl.num_programs(0) - 1))
        def _():
            tail_ref[...] = o_ref[MOE_TILE - tail_rows:, :]


def _moe(x, g, router, w1, w3, w2, final_g, final_norm, head_rows=None):
    t = x.shape[0]
    tail_rows = 0 if head_rows is None else t - head_rows
    assert tail_rows == 0 or (tail_rows <= MOE_TILE and tail_rows % SUBLANES == 0)
    r = jnp.arange(MOE_HALF)
    before = (r[None, :] < r[:, None]).astype(BF16)
    once = pl.Buffered(1)
    out_specs = [pl.BlockSpec((MOE_TILE, D_MODEL), lambda i, e, f: (i, 0), pipeline_mode=once)]
    out_shape = [jax.ShapeDtypeStruct((t - tail_rows, D_MODEL), F32)]
    if tail_rows:
        out_specs.append(pl.BlockSpec((tail_rows, D_MODEL), lambda i, e, f: (0, 0), pipeline_mode=once))
        out_shape.append(jax.ShapeDtypeStruct((tail_rows, D_MODEL), F32))
    return pl.pallas_call(
        functools.partial(_moe_kernel, final_norm=final_norm, tail_rows=tail_rows),
        grid=(t // MOE_TILE, N_EXPERTS, D_FF // FF_BLOCK),
        in_specs=[
            pl.BlockSpec((MOE_TILE, D_MODEL), lambda i, e, f: (i, 0), pipeline_mode=once),
            pl.BlockSpec((1, D_MODEL), lambda i, e, f: (0, 0)),
            pl.BlockSpec((D_MODEL, LANES), lambda i, e, f: (0, 0)),
            pl.BlockSpec((MOE_HALF, MOE_HALF), lambda i, e, f: (0, 0), pipeline_mode=once),
            pl.BlockSpec((1, D_MODEL, FF_BLOCK), lambda i, e, f: (e, 0, f)),
            pl.BlockSpec((1, D_MODEL, FF_BLOCK), lambda i, e, f: (e, 0, f)),
            pl.BlockSpec((1, FF_BLOCK, D_MODEL), lambda i, e, f: (e, f, 0)),
            pl.BlockSpec((1, D_MODEL), lambda i, e, f: (0, 0)),
        ],
        out_specs=out_specs,
        out_shape=out_shape,
        scratch_shapes=[
            pltpu.VMEM((MOE_TILE, D_MODEL), BF16),
            pltpu.VMEM((MOE_HALVES, MOE_HALF, LANES), F32),
            pltpu.VMEM((MOE_HALVES, MOE_HALF, LANES), F32),
            pltpu.VMEM((MOE_HALVES, LANES, MOE_HALF), F32),
            pltpu.VMEM((MOE_HALVES, SUBLANES, LANES), F32),
            pltpu.VMEM((MOE_HALVES * MOE_MAIN, D_MODEL), BF16),
            pltpu.VMEM((MOE_HALVES * MOE_MAIN, LANES), F32),
            pltpu.VMEM((MOE_HALVES * MOE_MAIN, D_MODEL), F32),
        ],
        compiler_params=_cparams(("arbitrary", "arbitrary", "arbitrary")),
        name="moe",
    )(x, g, router, before, w1, w3, w2, final_g)


def _pack_w_in(w):
    main = jnp.concatenate([
        w[:, OFF_BQKV:OFF_BQKV + DN_QKV],
        w[:, OFF_C:OFF_C + 3 * ATT_W],
        w[:, OFF_GATE:OFF_GATE + N_BRANCH * D_MODEL],
        w[:, OFF_A:OFF_A + 2 * CONV_CH],
        w[:, OFF_BG:OFF_BG + DN_HEADS * DN_DV],
    ], axis=1).astype(BF16)
    wa = w[:, OFF_BA:OFF_BA + DN_HEADS]
    wb = w[:, OFF_BB:OFF_BB + DN_HEADS]
    zc = jnp.zeros((D_MODEL, LANES - DN_HEADS), w.dtype)
    wab = jnp.concatenate([wa, zc, wb, zc], axis=1).astype(BF16)
    zr = jnp.zeros((SUBLANES - DN_HEADS, D_MODEL), w.dtype)
    wabt = jnp.concatenate([wa.T, zr, wb.T, zr], axis=0).astype(BF16)
    return main, wab, wabt


def _router_parts(router):
    parts = jnp.concatenate(_split3(router), axis=1)
    return jnp.pad(parts, ((0, 0), (0, LANES - parts.shape[1])))


def _lane_row(v):
    return jnp.pad(v.astype(F32), (0, LANES - v.shape[0]))[None, :]


def _sublane_col(v):
    return jnp.pad(v.astype(F32), (0, SUBLANES - v.shape[0]))[:, None]


def kernel(x_prompt, x_sample, cache_conv_a, state_sconv_b, state_delta_b, cache_k_c, cache_v_c, norm1_g, w_in, b_gate, dw_a, dwb_a, ln_a_g, ln_a_b, w_a_out, dw_b, a_log, dt_bias, onorm_b, w_b_out, rel_bias, w_c_out, w_out, norm2_g, ffn_w1, ffn_w3, ffn_w2, router, moe_w1, moe_w3, moe_w2, final_norm_g):
    bp, seq_len, _ = x_prompt.shape
    n_samp, samp_len, _ = x_sample.shape
    depth = w_in.shape[0]
    assert bp == 1 and samp_len == CHUNK and seq_len % TILE == 0 and (n_samp * CHUNK) % TILE == 0
    assert (seq_len + n_samp * CHUNK) % MOE_TILE == 0
    n_prompt_chunks = seq_len // CHUNK
    n_prompt = seq_len
    keep = min(ATT_PREV * CHUNK, seq_len)
    dt = x_prompt.dtype

    x = jnp.concatenate([x_prompt.reshape(seq_len, D_MODEL), x_sample.reshape(n_samp * CHUNK, D_MODEL)], axis=0)
    fg = final_norm_g[None, :]
    states = []
    for l in range(depth):
        w_main, wab, wabt = _pack_w_in(w_in[l])
        h, ab, abt = _in_proj(x, norm1_g[l][None, :], w_main, wab, wabt)

        hist_a = jnp.pad(jnp.concatenate([jnp.zeros((1,) + cache_conv_a.shape[2:], dt), cache_conv_a[l]], axis=0),
                         ((0, 0), (A_HIST - (CONV_WIDTH - 1), 0), (0, 0)))
        act_a, st_a = _conv_a(h, hist_a, dw_a[l], dwb_a[l][None, :], ln_a_g[l][None, :], ln_a_b[l][None, :],
                              n_prompt_chunks)

        hist_b = jnp.pad(jnp.concatenate([jnp.zeros((1,) + state_sconv_b.shape[2:], dt), state_sconv_b[l]], axis=0),
                         ((0, 0), (B_HIST - (DN_CONV - 1), 0), (0, 0)))
        s0 = jnp.concatenate([jnp.zeros((1,) + state_delta_b.shape[2:], dt), state_delta_b[l]], axis=0)
        act_b, st_sc, st_s = _delta(h, ab, abt, hist_b, s0, dw_b[l], _lane_row(a_log[l]), _lane_row(dt_bias[l]),
                                    _sublane_col(a_log[l]), _sublane_col(dt_bias[l]), onorm_b[l][None, :],
                                    n_prompt_chunks)

        bias = _bias_table(rel_bias[l])
        ck = cache_k_c[l].reshape(n_samp, -1, ATT_W).astype(BF16)
        cv = cache_v_c[l].reshape(n_samp, -1, ATT_W).astype(BF16)
        act_c = jnp.concatenate([_attn_prompt(h, bias, n_prompt_chunks),
                                 _attn_sample(h, ck, cv, bias, n_prompt_chunks)], axis=0)

        x = _merge(x, act_a, act_b, act_c, h, w_a_out[l].astype(BF16), w_b_out[l].astype(BF16),
                   w_c_out[l].astype(BF16), w_out[l].astype(BF16), b_gate[l])

        last = l == depth - 1
        j = l // 2
        if l % 2 == 0:
            x = _ffn(x, norm2_g[l][None, :], ffn_w1[j].astype(BF16), ffn_w3[j].astype(BF16),
                     ffn_w2[j].astype(BF16), fg, last)
        else:
            outs = _moe(x, norm2_g[l][None, :], _router_parts(router[j]),
                        moe_w1[j].astype(BF16), moe_w3[j].astype(BF16), moe_w2[j].astype(BF16), fg, last,
                        head_rows=n_prompt if last else None)
            x = outs[0] if len(outs) == 1 else outs

        k_new = h[:, H_C + ATT_W:H_C + 2 * ATT_W].astype(dt)
        v_new = h[:, H_C + 2 * ATT_W:H_C + 3 * ATT_W].astype(dt)
        states.append(dict(
            conv=st_a[:, A_HIST - (CONV_WIDTH - 1):, :],
            sconv=st_sc[:, B_HIST - (DN_CONV - 1):, :],
            delta=st_s,
            k=k_new, v=v_new))

    def stack(fn):
        return jnp.stack([fn(s) for s in states])

    heads = (ATT_HEADS, ATT_HD)
    y_head, y_tail = x if isinstance(x, (list, tuple)) else (x[:n_prompt], x[n_prompt:])
    y_prompt = y_head.reshape(1, seq_len, D_MODEL)
    y_sample = y_tail.reshape(n_samp, CHUNK, D_MODEL)
    return (
        y_prompt, y_sample,
        stack(lambda s: s["conv"][:1]), stack(lambda s: s["sconv"][:1]), stack(lambda s: s["delta"][:1]),
        stack(lambda s: s["k"][n_prompt - keep:n_prompt].reshape(1, keep, *heads)),
        stack(lambda s: s["v"][n_prompt - keep:n_prompt].reshape(1, keep, *heads)),
        stack(lambda s: s["conv"][1:]), stack(lambda s: s["sconv"][1:]), stack(lambda s: s["delta"][1:]),
        stack(lambda s: s["k"][n_prompt:].reshape(n_samp, CHUNK, *heads)),
        stack(lambda s: s["v"][n_prompt:].reshape(n_samp, CHUNK, *heads)),
    )
```

```python
import functools

import jax
import jax.numpy as jnp
from jax import lax
from jax.experimental import pallas as pl
from jax.experimental.pallas import tpu as pltpu

F32 = jnp.float32
BF16 = jnp.bfloat16

D_MODEL = 1024
CHUNK = 64
CONV_CH = 512
CONV_WIDTH = 31
DN_HEADS = 4
DN_DK = 128
DN_DV = 128
DN_CONV = 4
DN_QK = DN_HEADS * DN_DK
DN_QKV = DN_HEADS * (2 * DN_DK + DN_DV)
ATT_HEADS = 8
ATT_HD = 64
ATT_W = ATT_HEADS * ATT_HD
ATT_PREV = 8
BAND = (ATT_PREV + 1) * CHUNK
REL_MAX = 128
REL_MIN = -(CHUNK - 1)
N_REL = REL_MAX - REL_MIN + 1
D_FF = 2816
N_EXPERTS = 8
N_BRANCH = 3
EPS = 1e-6
NEG_INF = -1e30

OFF_A = 0
OFF_BQKV = OFF_A + 2 * CONV_CH
OFF_BA = OFF_BQKV + DN_QKV
OFF_BB = OFF_BA + DN_HEADS
OFF_BG = OFF_BB + DN_HEADS
OFF_C = OFF_BG + DN_HEADS * DN_DV
OFF_GATE = OFF_C + 3 * ATT_W

H_BQKV = 0
H_C = 1536
H_GATE = 3072
H_A = 6144
H_BG = 7168
H_COLS = 7680
H_BLOCK = 1536

LANES = 128
SUBLANES = 8
TILE = 512
CPT = TILE // CHUNK
FF_BLOCK = D_FF // 2
FF_CHUNK = 256
VMEM_LIMIT = 48 * 1024 * 1024


def _cparams(sem):
    return pltpu.CompilerParams(dimension_semantics=sem, vmem_limit_bytes=VMEM_LIMIT)


def _split3(x):
    hi = x.astype(BF16)
    r1 = x - hi.astype(F32)
    mid = r1.astype(BF16)
    lo = (r1 - mid.astype(F32)).astype(BF16)
    return hi, mid, lo


def _dot(a, b):
    return jnp.dot(a, b, preferred_element_type=F32)


def _dot_nt(a, b):
    return lax.dot_general(a, b, (((1,), (1,)), ((), ())), preferred_element_type=F32)


def _dot_tn(a, b):
    return lax.dot_general(a, b, (((0,), (0,)), ((), ())), preferred_element_type=F32)


def _dot_sel_left(sel_bf16, x):
    hi, mid, lo = _split3(x)
    return _dot(sel_bf16, hi) + _dot(sel_bf16, mid) + _dot(sel_bf16, lo)


def _dot_sel_right(x, sel_bf16):
    hi, mid, lo = _split3(x)
    return _dot(hi, sel_bf16) + _dot(mid, sel_bf16) + _dot(lo, sel_bf16)


def _sigmoid(x):
    return 1.0 / (1.0 + jnp.exp(-x))


def _silu(x):
    return x * _sigmoid(x)


def _stream_specs(parts, width, n_first):
    if len(parts) == 1:
        return [pl.BlockSpec((TILE, width), lambda i: (i, 0))]
    return [pl.BlockSpec((TILE, width), lambda i: (jnp.minimum(i, n_first - 1), 0)),
            pl.BlockSpec((TILE, width), lambda i: (jnp.maximum(i - n_first, 0), 0))]


def _stream_tile(refs, n_first):
    if len(refs) == 1:
        return refs[0][...]
    return jnp.where(pl.program_id(0) < n_first, refs[0][...], refs[1][...])


def _in_proj_kernel(*refs, n_x, n_first):
    x_refs, (g_ref, w_ref, wab_ref, wabt_ref, h_ref, ab_ref, abt_ref) = refs[:n_x], refs[n_x:]
    x = _stream_tile(x_refs, n_first)
    xnb = (x * lax.rsqrt(jnp.mean(x * x, axis=-1, keepdims=True) + EPS) * g_ref[...]).astype(BF16)
    ab_ref[...] = _dot(xnb, wab_ref[...])
    abt_ref[...] = _dot_nt(wabt_ref[...], xnb)
    for j in range(H_COLS // H_BLOCK):
        cols = slice(j * H_BLOCK, (j + 1) * H_BLOCK)
        h_ref[:, cols] = _dot(xnb, w_ref[:, cols]).astype(BF16)


def _in_proj(x_parts, g, w, wab, wabt, n_first):
    t = sum(p.shape[0] for p in x_parts)
    once = pl.Buffered(1)
    return pl.pallas_call(
        functools.partial(_in_proj_kernel, n_x=len(x_parts), n_first=n_first),
        grid=(t // TILE,),
        in_specs=[
            *_stream_specs(x_parts, D_MODEL, n_first),
            pl.BlockSpec((1, D_MODEL), lambda i: (0, 0)),
            pl.BlockSpec((D_MODEL, H_COLS), lambda i: (0, 0), pipeline_mode=once),
            pl.BlockSpec((D_MODEL, 2 * LANES), lambda i: (0, 0)),
            pl.BlockSpec((2 * SUBLANES, D_MODEL), lambda i: (0, 0)),
        ],
        out_specs=[
            pl.BlockSpec((TILE, H_COLS), lambda i: (i, 0)),
            pl.BlockSpec((TILE, 2 * LANES), lambda i: (i, 0)),
            pl.BlockSpec((2 * SUBLANES, TILE), lambda i: (0, i)),
        ],
        out_shape=[
            jax.ShapeDtypeStruct((t, H_COLS), BF16),
            jax.ShapeDtypeStruct((t, 2 * LANES), F32),
            jax.ShapeDtypeStruct((2 * SUBLANES, t), F32),
        ],
        compiler_params=_cparams(("parallel",)),
        name="in_proj",
    )(*x_parts, g, w, wab, wabt)


A_HIST = 32


def _chunk_flags(c, n_prompt_chunks):
    is_start = jnp.logical_or(c == 0, c >= n_prompt_chunks)
    is_end = c >= n_prompt_chunks - 1
    seq = jnp.maximum(c - (n_prompt_chunks - 1), 0)
    return is_start, is_end, seq


def _conv_a_kernel(h_ref, hist_ref, dw_ref, dwb_ref, lng_ref, lnb_ref, act_ref, st_ref,
                   xp_ref, cv_ref, *, n_prompt_chunks):
    i = pl.program_id(0)
    hv = h_ref[...]
    xp_ref[A_HIST:A_HIST + TILE, :] = (
        hv[:, :CONV_CH].astype(F32) * _sigmoid(hv[:, CONV_CH:].astype(F32)))
    first_tap = A_HIST - (CONV_WIDTH - 1)
    for j in range(CPT):
        base = CHUNK * j
        is_start, is_end, seq = _chunk_flags(CPT * i + j, n_prompt_chunks)

        @pl.when(is_start)
        def _():
            xp_ref[base:base + A_HIST, :] = hist_ref[seq]

        for grp in range(CONV_CH // LANES):
            cols = slice(grp * LANES, (grp + 1) * LANES)
            y = None
            for b in range(SUBLANES):
                n_rows = CHUNK if b == 0 else CHUNK + SUBLANES
                z = None
                for a in range((first_tap + CONV_WIDTH - 1) // SUBLANES + 1):
                    k = SUBLANES * a + b - first_tap
                    if 0 <= k < CONV_WIDTH:
                        row0 = base + SUBLANES * a
                        term = xp_ref[row0:row0 + n_rows, cols] * dw_ref[k:k + 1, cols]
                        z = term if z is None else z + term
                zb = z[b:b + CHUNK]
                y = zb if y is None else y + zb
            cv_ref[base:base + CHUNK, cols] = y

        @pl.when(is_end)
        def _():
            st_ref[seq] = xp_ref[base + CHUNK:base + CHUNK + A_HIST, :]

    xp_ref[0:A_HIST, :] = xp_ref[TILE:TILE + A_HIST, :]
    y = cv_ref[...] + dwb_ref[...]
    yc = y - jnp.mean(y, axis=-1, keepdims=True)
    yn = yc * lax.rsqrt(jnp.mean(yc * yc, axis=-1, keepdims=True) + EPS)
    act_ref[...] = _silu(yn * lng_ref[...] + lnb_ref[...]).astype(BF16)


def _conv_a(h, hist, dw, dwb, lng, lnb, n_prompt_chunks):
    t = h.shape[0]
    n_seq = hist.shape[0]
    full = lambda *shape: pl.BlockSpec(shape, lambda i: (0,) * len(shape))
    return pl.pallas_call(
        functools.partial(_conv_a_kernel, n_prompt_chunks=n_prompt_chunks),
        grid=(t // TILE,),
        in_specs=[
            pl.BlockSpec((TILE, 2 * CONV_CH), lambda i: (i, H_A // (2 * CONV_CH))),
            full(n_seq, A_HIST, CONV_CH),
            full(CONV_WIDTH, CONV_CH),
            full(1, CONV_CH), full(1, CONV_CH), full(1, CONV_CH),
        ],
        out_specs=[
            pl.BlockSpec((TILE, CONV_CH), lambda i: (i, 0)),
            full(n_seq, A_HIST, CONV_CH),
        ],
        out_shape=[
            jax.ShapeDtypeStruct((t, CONV_CH), BF16),
            jax.ShapeDtypeStruct((n_seq, A_HIST, CONV_CH), F32),
        ],
        scratch_shapes=[
            pltpu.VMEM((A_HIST + TILE, CONV_CH), F32),
            pltpu.VMEM((TILE, CONV_CH), F32),
        ],
        compiler_params=_cparams(("arbitrary",)),
        name="conv_a",
    )(h, hist, dw, dwb, lng, lnb)


B_HIST = SUBLANES


def _delta_kernel(qkv_ref, ab_ref, abt_ref, gate_ref, hist_ref, s0_ref, dw_ref,
                  alog_r_ref, dtb_r_ref, alog_c_ref, dtb_c_ref, onorm_ref, sel_ref,
                  ob_ref, sc_out_ref, s_out_ref,
                  xq_ref, pc_ref, qb_ref, kb_ref, kbeta_ref, vb_ref, kbe_ref, qg_ref, kd_ref,
                  gcx_ref, dl_ref, u_ref, w_ref, qk_ref, s_ref, *, n_prompt_chunks):
    i = pl.program_id(0)
    first_tap = B_HIST - (DN_CONV - 1)

    @pl.when(i == 0)
    def _():
        s_ref[...] = jnp.zeros(s_ref.shape, F32)

    xq_ref[B_HIST:B_HIST + TILE, :] = qkv_ref[...].astype(F32)
    for j in range(CPT):
        base = CHUNK * j
        is_start, is_end, seq = _chunk_flags(CPT * i + j, n_prompt_chunks)

        @pl.when(is_start)
        def _():
            xq_ref[base:base + B_HIST, :] = hist_ref[seq]

        for grp in range(DN_QKV // LANES):
            cols = slice(grp * LANES, (grp + 1) * LANES)
            acc = jnp.zeros((CHUNK, LANES), F32)
            for k in range(DN_CONV):
                off = base + first_tap + k
                acc = acc + xq_ref[off:off + CHUNK, cols] * dw_ref[k:k + 1, cols]
            pc_ref[base:base + CHUNK, cols] = acc

        @pl.when(is_end)
        def _():
            sc_out_ref[seq] = xq_ref[base + CHUNK:base + CHUNK + B_HIST, :]

    xq_ref[0:B_HIST, :] = xq_ref[TILE:TILE + B_HIST, :]

    for grp in range(DN_QKV // LANES):
        cols = slice(grp * LANES, (grp + 1) * LANES)
        x = _silu(pc_ref[:, cols])
        if grp < 2 * DN_HEADS:
            x = x * lax.rsqrt(jnp.sum(x * x, axis=-1, keepdims=True) + EPS)
        if grp < DN_HEADS:
            x = x * (DN_DK ** -0.5)
        pc_ref[:, cols] = x

    def log_decay(alpha, alog, dtb):
        z = alpha + dtb
        softplus = jnp.maximum(z, 0.0) + jnp.log(1.0 + jnp.exp(-jnp.abs(z)))
        return -jnp.exp(alog) * softplus

    ab = ab_ref[...]
    g_col = log_decay(ab[:, :LANES], alog_r_ref[...], dtb_r_ref[...])
    beta = _sigmoid(ab[:, LANES:])
    abt = abt_ref[...]
    g_row = log_decay(abt[:SUBLANES, :], alog_c_ref[...], dtb_c_ref[...])
    gcc = _dot_sel_left(sel_ref[0], g_col)
    gc_row = _dot_sel_right(g_row, sel_ref[1])
    gcl = _dot_sel_left(sel_ref[2], gcc)

    def expansion(width):
        hr = lax.broadcasted_iota(jnp.int32, (LANES, DN_HEADS * width), 0)
        hc = lax.broadcasted_iota(jnp.int32, (LANES, DN_HEADS * width), 1)
        return jnp.where(hc // width == hr, 1.0, 0.0).astype(BF16)

    wide = expansion(DN_DK)
    beta_x = _dot_sel_right(beta, wide)
    eg_x = _dot_sel_right(jnp.exp(gcc), wide)
    ekd_x = _dot_sel_right(jnp.exp(gcl - gcc), wide)
    dl_ref[...] = _dot_sel_right(jnp.exp(gcl), wide)
    gcx_ref[...] = _dot_sel_right(gcc, expansion(CHUNK))

    q = pc_ref[:, 0:DN_QK]
    k = pc_ref[:, DN_QK:2 * DN_QK]
    kbeta = k * beta_x
    qb_ref[...] = q.astype(BF16)
    kb_ref[...] = k.astype(BF16)
    kbeta_ref[...] = kbeta.astype(BF16)
    vb_ref[...] = (pc_ref[:, 2 * DN_QK:] * beta_x).astype(BF16)
    kbe_ref[...] = (kbeta * eg_x).astype(BF16)
    qg_ref[...] = (q * eg_x).astype(BF16)
    kd_ref[...] = (k * ekd_x).astype(BF16)

    def block_diag(y, width):
        blk = lax.broadcasted_iota(jnp.int32, y.shape, 1) // width
        zero = jnp.zeros((), y.dtype)
        return jnp.concatenate([jnp.where(blk == h, y, zero) for h in range(DN_HEADS)], axis=0)

    row4 = lax.broadcasted_iota(jnp.int32, (CHUNK, DN_HEADS * CHUNK), 0)
    col4 = lax.broadcasted_iota(jnp.int32, (CHUNK, DN_HEADS * CHUNK), 1) % CHUNK
    eye4 = jnp.where(row4 == col4, 1.0, 0.0).astype(F32)

    chunk_rows = [slice(CHUNK * j, CHUNK * (j + 1)) for j in range(CPT)]
    n_pow, t_inv = [None] * CPT, [None] * CPT
    for j, rows in enumerate(chunk_rows):
        lhs = jnp.concatenate([kbeta_ref[rows, :], qb_ref[rows, :]], axis=0)
        both = _dot_nt(lhs, block_diag(kb_ref[rows, :], DN_DK))
        gc_r = jnp.concatenate([gc_row[h:h + 1, rows] for h in range(DN_HEADS)], axis=1)
        gam = jnp.exp(jnp.where(row4 >= col4, gcx_ref[rows, :] - gc_r, -jnp.inf))
        qk_ref[j] = (both[CHUNK:] * gam).astype(BF16)
        n_pow[j] = jnp.where(row4 > col4, -(both[:CHUNK] * gam), 0.0)
        t_inv[j] = eye4 + n_pow[j]
    for j in range(CPT):
        nb = n_pow[j].astype(BF16)
        n_pow[j] = _dot(nb, block_diag(nb, CHUNK))
    for _ in range(4):
        for j in range(CPT):
            nb = n_pow[j].astype(BF16)
            res = _dot(jnp.concatenate([nb, t_inv[j].astype(BF16)], axis=0), block_diag(nb, CHUNK))
            n_pow[j] = res[:CHUNK]
            t_inv[j] = t_inv[j] + res[CHUNK:]
    for j, rows in enumerate(chunk_rows):
        t_fin = t_inv[j] + _dot(t_inv[j].astype(BF16), block_diag(n_pow[j].astype(BF16), CHUNK))
        tb = t_fin.astype(BF16)
        u_ref[rows, :] = _dot(tb, block_diag(vb_ref[rows, :], DN_DV))
        w_ref[rows, :] = _dot(tb, block_diag(kbe_ref[rows, :], DN_DK)).astype(BF16)

    for j, rows in enumerate(chunk_rows):
        is_start, _, seq = _chunk_flags(CPT * i + j, n_prompt_chunks)
        vns, outs = [], []
        for h in range(DN_HEADS):
            hc = slice(h * DN_DK, (h + 1) * DN_DK)
            s = jnp.where(is_start, s0_ref[seq, h], s_ref[h])
            res = _dot(jnp.concatenate([w_ref[rows, hc], qg_ref[rows, hc]], axis=0), s.astype(BF16))
            vnb = (u_ref[rows, hc] - res[:CHUNK]).astype(BF16)
            s_new = s * dl_ref[CHUNK * j:CHUNK * j + 1, hc] + _dot_tn(kd_ref[rows, hc], vnb)
            s_ref[h] = s_new
            s_out_ref[seq, h] = s_new
            vns.append(vnb)
            outs.append(res[CHUNK:])
        vn_all = jnp.concatenate(vns, axis=1)
        o_all = jnp.concatenate(outs, axis=1) + _dot(qk_ref[j], block_diag(vn_all, DN_DV))
        for h in range(DN_HEADS):
            hc = slice(h * DN_DV, (h + 1) * DN_DV)
            o = o_all[:, hc]
            on = o * lax.rsqrt(jnp.mean(o * o, axis=-1, keepdims=True) + EPS) * onorm_ref[...]
            ob_ref[rows, hc] = (on * _silu(gate_ref[rows, hc].astype(F32))).astype(BF16)


def _chunk_selectors():
    r = jnp.arange(TILE)[:, None]
    c = jnp.arange(TILE)[None, :]
    same_chunk = (r // CHUNK) == (c // CHUNK)
    low = same_chunk & (c <= r)
    upp = same_chunk & (r <= c)
    last = c == (r // CHUNK) * CHUNK + (CHUNK - 1)
    return jnp.stack([low, upp, last]).astype(BF16)


def _delta(h, ab, abt, hist, s0, dw, alog_r, dtb_r, alog_c, dtb_c, onorm, n_prompt_chunks):
    t = h.shape[0]
    n_seq = hist.shape[0]
    full = lambda *shape: pl.BlockSpec(shape, lambda i: (0,) * len(shape))
    return pl.pallas_call(
        functools.partial(_delta_kernel, n_prompt_chunks=n_prompt_chunks),
        grid=(t // TILE,),
        in_specs=[
            pl.BlockSpec((TILE, DN_QKV), lambda i: (i, H_BQKV // DN_QKV)),
            pl.BlockSpec((TILE, 2 * LANES), lambda i: (i, 0)),
            pl.BlockSpec((2 * SUBLANES, TILE), lambda i: (0, i)),
            pl.BlockSpec((TILE, DN_HEADS * DN_DV), lambda i: (i, H_BG // (DN_HEADS * DN_DV))),
            full(n_seq, B_HIST, DN_QKV),
            full(n_seq, DN_HEADS, DN_DK, DN_DV),
            full(DN_CONV, DN_QKV),
            full(1, LANES), full(1, LANES), full(SUBLANES, 1), full(SUBLANES, 1),
            full(1, DN_DV),
            full(3, TILE, TILE),
        ],
        out_specs=[
            pl.BlockSpec((TILE, DN_HEADS * DN_DV), lambda i: (i, 0)),
            full(n_seq, B_HIST, DN_QKV),
            full(n_seq, DN_HEADS, DN_DK, DN_DV),
        ],
        out_shape=[
            jax.ShapeDtypeStruct((t, DN_HEADS * DN_DV), BF16),
            jax.ShapeDtypeStruct((n_seq, B_HIST, DN_QKV), F32),
            jax.ShapeDtypeStruct((n_seq, DN_HEADS, DN_DK, DN_DV), F32),
        ],
        scratch_shapes=[
            pltpu.VMEM((B_HIST + TILE, DN_QKV), F32),
            pltpu.VMEM((TILE, DN_QKV), F32),
            *[pltpu.VMEM((TILE, DN_QK), BF16) for _ in range(7)],
            pltpu.VMEM((TILE, DN_HEADS * CHUNK), F32),
            pltpu.VMEM((TILE, DN_QK), F32),
            pltpu.VMEM((TILE, DN_HEADS * DN_DV), F32),
            pltpu.VMEM((TILE, DN_QK), BF16),
            pltpu.VMEM((CPT, CHUNK, DN_HEADS * CHUNK), BF16),
            pltpu.VMEM((DN_HEADS, DN_DK, DN_DV), F32),
        ],
        compiler_params=_cparams(("arbitrary",)),
        name="delta",
    )(h, ab, abt, h, hist, s0, dw, alog_r, dtb_r, alog_c, dtb_c, onorm, _chunk_selectors())


PAIR = 2 * CHUNK
PAIR_BAND = (ATT_PREV + 2) * CHUNK


def _bias_table_kernel(rb_ref, out_ref):
    rel = lax.broadcasted_iota(jnp.int32, (2 * LANES, PAIR_BAND), 0)
    key = lax.broadcasted_iota(jnp.int32, (2 * LANES, PAIR_BAND), 1)
    key_chunk = lax.broadcasted_iota(jnp.int32, (ATT_HEADS, PAIR_BAND), 1) // CHUNK
    rb = rb_ref[...]
    for a in range(PAIR):
        idx = jnp.clip(ATT_PREV * CHUNK + a - key, REL_MIN, REL_MAX) - REL_MIN
        onehot = jnp.where(rel == idx, 1.0, 0.0).astype(BF16)
        band_pos = key_chunk - a // CHUNK
        seen = jnp.logical_and(band_pos >= 0, band_pos <= ATT_PREV)
        out_ref[:, a, :] = jnp.where(seen, _dot_sel_right(rb, onehot), NEG_INF)


def _bias_table(rel_bias):
    rb = jnp.pad(rel_bias, ((0, 0), (0, 2 * LANES - N_REL)))
    return pl.pallas_call(
        _bias_table_kernel,
        out_shape=jax.ShapeDtypeStruct((ATT_HEADS, PAIR, PAIR_BAND), F32),
        compiler_params=pltpu.CompilerParams(vmem_limit_bytes=VMEM_LIMIT),
        name="bias_table",
    )(rb)


def _attend(q, kband_ref, vband_ref, row0, n_keys, bias_of_head, first_valid_col):
    n_q = q.shape[0]
    lane = lax.broadcasted_iota(jnp.int32, (n_q, LANES), 1)
    valid = lax.broadcasted_iota(jnp.int32, (n_q, n_keys), 1) >= first_valid_col
    zero = jnp.zeros((), BF16)
    qs = q * jnp.asarray(ATT_HD ** -0.5, BF16)
    heads = range(ATT_HEADS)
    col_of = lambda h: slice((h // 2) * LANES, (h // 2 + 1) * LANES)
    s = []
    for h in heads:
        in_head = (lane < ATT_HD) if h % 2 == 0 else (lane >= ATT_HD)
        k2 = kband_ref[pl.ds(row0, n_keys), col_of(h)]
        s.append(_dot_nt(jnp.where(in_head, qs[:, col_of(h)], zero), k2))
    s = [jnp.where(valid, s[h] + bias_of_head(h), NEG_INF) for h in heads]
    top = [jnp.max(s[h], axis=-1, keepdims=True) for h in heads]
    p = [jnp.exp(s[h] - top[h]) for h in heads]
    denom = [jnp.sum(p[h], axis=-1, keepdims=True) for h in heads]
    o = [_dot(p[h].astype(BF16), vband_ref[pl.ds(row0, n_keys), col_of(h)]) for h in heads]
    o = [o[h] / denom[h] for h in heads]
    return [jnp.where(lane < ATT_HD, o[2 * hp], o[2 * hp + 1]) for hp in range(ATT_HEADS // 2)]


def _attn_prompt_kernel(q_ref, kp_ref, kc_ref, vp_ref, vc_ref, bias_ref, o_ref, kb_ref, vb_ref):
    i = pl.program_id(0)
    kb_ref[0:TILE, :] = kp_ref[...]
    kb_ref[TILE:2 * TILE, :] = kc_ref[...]
    vb_ref[0:TILE, :] = vp_ref[...]
    vb_ref[TILE:2 * TILE, :] = vc_ref[...]

    def pair_body(p, carry):
        base = pl.multiple_of(p * PAIR, PAIR)
        rows = pl.ds(base, PAIR)
        first_valid = jnp.maximum(ATT_PREV - (CPT * i + 2 * p), 0) * CHUNK
        outs = _attend(q_ref[rows, :], kb_ref, vb_ref, base, PAIR_BAND, lambda h: bias_ref[h], first_valid)
        for hp, o in enumerate(outs):
            o_ref[rows, hp * LANES:(hp + 1) * LANES] = o.astype(BF16)
        return carry

    lax.fori_loop(0, TILE // PAIR, pair_body, 0)


def _attn_prompt(h, bias, n_prompt_chunks):
    n_tiles = n_prompt_chunks // CPT
    qb, kb, vb = (H_C // ATT_W, H_C // ATT_W + 1, H_C // ATT_W + 2)
    prev = lambda i: jnp.maximum(i - 1, 0)
    return pl.pallas_call(
        _attn_prompt_kernel,
        grid=(n_tiles,),
        in_specs=[
            pl.BlockSpec((TILE, ATT_W), lambda i: (i, qb)),
            pl.BlockSpec((TILE, ATT_W), lambda i: (prev(i), kb)),
            pl.BlockSpec((TILE, ATT_W), lambda i: (i, kb)),
            pl.BlockSpec((TILE, ATT_W), lambda i: (prev(i), vb)),
            pl.BlockSpec((TILE, ATT_W), lambda i: (i, vb)),
            pl.BlockSpec((ATT_HEADS, PAIR, PAIR_BAND), lambda i: (0, 0, 0)),
        ],
        out_specs=pl.BlockSpec((TILE, ATT_W), lambda i: (i, 0)),
        out_shape=jax.ShapeDtypeStruct((n_tiles * TILE, ATT_W), BF16),
        scratch_shapes=[pltpu.VMEM((2 * TILE, ATT_W), BF16), pltpu.VMEM((2 * TILE, ATT_W), BF16)],
        compiler_params=_cparams(("arbitrary",)),
        name="attn_prompt",
    )(h, h, h, h, h, bias)


def _attn_sample_kernel(q_ref, k_ref, v_ref, ck_ref, cv_ref, bias_ref, o_ref, kb_ref, vb_ref):
    kb_ref[0:ATT_PREV * CHUNK, :] = ck_ref[0]
    kb_ref[ATT_PREV * CHUNK:BAND, :] = k_ref[...]
    vb_ref[0:ATT_PREV * CHUNK, :] = cv_ref[0]
    vb_ref[ATT_PREV * CHUNK:BAND, :] = v_ref[...]
    outs = _attend(q_ref[...], kb_ref, vb_ref, 0, BAND, lambda h: bias_ref[h, 0:CHUNK, 0:BAND], 0)
    for hp, o in enumerate(outs):
        o_ref[:, hp * LANES:(hp + 1) * LANES] = o.astype(BF16)


def _attn_sample(h, cache_k, cache_v, bias, n_prompt_chunks):
    n_seq = cache_k.shape[0]
    qb, kb, vb = (H_C // ATT_W, H_C // ATT_W + 1, H_C // ATT_W + 2)
    return pl.pallas_call(
        _attn_sample_kernel,
        grid=(n_seq,),
        in_specs=[
            pl.BlockSpec((CHUNK, ATT_W), lambda b: (n_prompt_chunks + b, qb)),
            pl.BlockSpec((CHUNK, ATT_W), lambda b: (n_prompt_chunks + b, kb)),
            pl.BlockSpec((CHUNK, ATT_W), lambda b: (n_prompt_chunks + b, vb)),
            pl.BlockSpec((1, ATT_PREV * CHUNK, ATT_W), lambda b: (b, 0, 0)),
            pl.BlockSpec((1, ATT_PREV * CHUNK, ATT_W), lambda b: (b, 0, 0)),
            pl.BlockSpec((ATT_HEADS, PAIR, PAIR_BAND), lambda b: (0, 0, 0)),
        ],
        out_specs=pl.BlockSpec((CHUNK, ATT_W), lambda b: (b, 0)),
        out_shape=jax.ShapeDtypeStruct((n_seq * CHUNK, ATT_W), BF16),
        scratch_shapes=[pltpu.VMEM((BAND, ATT_W), BF16), pltpu.VMEM((BAND, ATT_W), BF16)],
        compiler_params=_cparams(("arbitrary",)),
        name="attn_sample",
    )(h, h, h, cache_k, cache_v, bias)


def _merge_kernel(*refs, n_x, n_c, n_first):
    x_refs, refs = refs[:n_x], refs[n_x:]
    a_ref, b_ref = refs[:2]
    c_refs, refs = refs[2:2 + n_c], refs[2 + n_c:]
    gate_ref, wa_ref, wb_ref, wc_ref, wo_ref, bg_ref, o_ref = refs
    acts = (a_ref[...], b_ref[...], _stream_tile(c_refs, n_first))
    merged = jnp.zeros((TILE, D_MODEL), F32)
    for n, (act, w_ref) in enumerate(zip(acts, (wa_ref, wb_ref, wc_ref))):
        y = _dot(act, w_ref[...])
        z = gate_ref[:, n * D_MODEL:(n + 1) * D_MODEL].astype(F32) + bg_ref[n:n + 1, :]
        merged = merged + _sigmoid(z) * y
    o_ref[...] = _stream_tile(x_refs, n_first) + _dot(merged.astype(BF16), wo_ref[...])


def _merge(x_parts, act_a, act_b, act_c_parts, h, wa, wb, wc, wo, b_gate, n_first):
    t = h.shape[0]
    full = lambda *shape: pl.BlockSpec(shape, lambda i: (0,) * len(shape))
    act = pl.BlockSpec((TILE, CONV_CH), lambda i: (i, 0))
    return pl.pallas_call(
        functools.partial(_merge_kernel, n_x=len(x_parts), n_c=len(act_c_parts), n_first=n_first),
        grid=(t // TILE,),
        in_specs=[
            *_stream_specs(x_parts, D_MODEL, n_first),
            act, act,
            *_stream_specs(act_c_parts, ATT_W, n_first),
            pl.BlockSpec((TILE, N_BRANCH * D_MODEL), lambda i: (i, H_GATE // (N_BRANCH * D_MODEL))),
            full(CONV_CH, D_MODEL), full(DN_HEADS * DN_DV, D_MODEL), full(ATT_W, D_MODEL),
            full(D_MODEL, D_MODEL), full(N_BRANCH, D_MODEL),
        ],
        out_specs=pl.BlockSpec((TILE, D_MODEL), lambda i: (i, 0)),
        out_shape=jax.ShapeDtypeStruct((t, D_MODEL), F32),
        compiler_params=_cparams(("parallel",)),
        name="merge",
    )(*x_parts, act_a, act_b, *act_c_parts, h, wa, wb, wc, wo, b_gate)


def _rms(x, g):
    return x * lax.rsqrt(jnp.mean(x * x, axis=-1, keepdims=True) + EPS) * g


def _ffn_kernel(x_ref, g_ref, w1_ref, w3_ref, w2_ref, fg_ref, o_ref, *, final_norm):
    x = x_ref[...]
    xn = _rms(x, g_ref[...]).astype(BF16)
    y = x
    for f in range(D_FF // FF_CHUNK):
        cols = slice(f * FF_CHUNK, (f + 1) * FF_CHUNK)
        hidden = _silu(_dot(xn, w1_ref[:, cols])) * _dot(xn, w3_ref[:, cols])
        y = y + _dot(hidden.astype(BF16), w2_ref[cols, :])
    o_ref[...] = _rms(y, fg_ref[...]) if final_norm else y


def _ffn(x, g, w1, w3, w2, final_g, final_norm):
    t = x.shape[0]
    once = pl.Buffered(1)
    return pl.pallas_call(
        functools.partial(_ffn_kernel, final_norm=final_norm),
        grid=(t // TILE,),
        in_specs=[
            pl.BlockSpec((TILE, D_MODEL), lambda i: (i, 0)),
            pl.BlockSpec((1, D_MODEL), lambda i: (0, 0)),
            pl.BlockSpec((D_MODEL, D_FF), lambda i: (0, 0), pipeline_mode=once),
            pl.BlockSpec((D_MODEL, D_FF), lambda i: (0, 0), pipeline_mode=once),
            pl.BlockSpec((D_FF, D_MODEL), lambda i: (0, 0), pipeline_mode=once),
            pl.BlockSpec((1, D_MODEL), lambda i: (0, 0)),
        ],
        out_specs=pl.BlockSpec((TILE, D_MODEL), lambda i: (i, 0)),
        out_shape=jax.ShapeDtypeStruct((t, D_MODEL), F32),
        compiler_params=_cparams(("parallel",)),
        name="ffn",
    )(x, g, w1, w3, w2, final_g)


MOE_HALF = 768
MOE_HALVES = 2
MOE_TILE = MOE_HALF * MOE_HALVES
MOE_MAIN = 224
MOE_OVER = 128


def _moe_kernel(x_ref, g_ref, r_ref, before_ref, w1_ref, w3_ref, w2_ref, fg_ref, o_ref, *rest,
                final_norm, tail_rows):
    if tail_rows:
        tail_ref, *rest = rest
    xn_ref, gates_ref, key_ref, keyt_ref, cnt_ref, xc_ref, gs_ref, yc_ref = rest
    e = pl.program_id(1)
    f = pl.program_id(2)
    last_f = pl.num_programs(2) - 1
    lane = lax.broadcasted_iota(jnp.int32, (MOE_HALF, LANES), 1)
    half_rows = [slice(hh * MOE_HALF, (hh + 1) * MOE_HALF) for hh in range(MOE_HALVES)]
    main_rows = [slice(hh * MOE_MAIN, (hh + 1) * MOE_MAIN) for hh in range(MOE_HALVES)]

    @pl.when(jnp.logical_and(e == 0, f == 0))
    def _():
        o_ref[...] = x_ref[...]
        r3 = r_ref[...]
        for hh, rows in enumerate(half_rows):
            xn = _rms(x_ref[rows, :], g_ref[...])
            xn_ref[rows, :] = xn.astype(BF16)
            xh, xm, xl = _split3(xn)
            ph, pm, pl_ = _dot(xh, r3), _dot(xm, r3), _dot(xl, r3)
            down = lambda p, k: pltpu.roll(p, LANES - k * N_EXPERTS, axis=1)
            logits = (ph + (down(ph, 1) + pm)) + (down(ph, 2) + down(pm, 1) + pl_)
            logits = jnp.where(lane < N_EXPERTS, logits, -jnp.inf)
            m1 = jnp.max(logits, axis=-1, keepdims=True)
            i1 = jnp.min(jnp.where(logits == m1, lane, LANES), axis=-1, keepdims=True)
            others = jnp.where(lane == i1, -jnp.inf, logits)
            m2 = jnp.max(others, axis=-1, keepdims=True)
            i2 = jnp.min(jnp.where(others == m2, lane, LANES), axis=-1, keepdims=True)
            e2 = jnp.exp(m2 - m1)
            w_top = 1.0 / (1.0 + e2)
            gates_ref[hh] = jnp.where(lane == i1, w_top, 0.0) + jnp.where(lane == i2, e2 * w_top, 0.0)
            routed = jnp.where(lane == i1, 1.0, jnp.where(lane == i2, 1.0, 0.0))
            key = jnp.where(routed > 0.0, _dot(before_ref[...], routed.astype(BF16)), -1.0)
            key_ref[hh] = key
            keyt_ref[hh] = key.T
            cnt_ref[hh] = jnp.broadcast_to(jnp.sum(routed, axis=0, keepdims=True), (SUBLANES, LANES))

    def pick_of(hh, first, n_rows):
        slot = lax.broadcasted_iota(jnp.int32, (n_rows, MOE_HALF), 0).astype(F32)
        return jnp.where(keyt_ref[hh, pl.ds(e, 1), :] - first == slot, 1.0, 0.0).astype(BF16)

    def place_of(hh, first, n_rows):
        key_col = jnp.sum(jnp.where(lane == e, key_ref[hh], 0.0), axis=-1, keepdims=True)
        slot = lax.broadcasted_iota(jnp.int32, (MOE_HALF, n_rows), 1).astype(F32)
        return jnp.where(key_col - first == slot, 1.0, 0.0).astype(BF16)

    def expert(xc, gs):
        lane_r = lax.broadcasted_iota(jnp.int32, gs.shape, 1)
        ge = jnp.sum(jnp.where(lane_r == e, gs, 0.0), axis=-1, keepdims=True)
        hidden = _silu(_dot(xc, w1_ref[0])) * _dot(xc, w3_ref[0])
        return _dot((hidden * ge).astype(BF16), w2_ref[0])

    @pl.when(f == 0)
    def _():
        for hh in range(MOE_HALVES):
            pick = pick_of(hh, 0.0, MOE_MAIN)
            xc_ref[main_rows[hh], :] = _dot(pick, xn_ref[half_rows[hh], :]).astype(BF16)
            gs_ref[main_rows[hh], :] = _dot_sel_left(pick, gates_ref[hh])

    contrib = expert(xc_ref[...], gs_ref[...])

    assert D_FF // FF_BLOCK >= 2

    @pl.when(f == 0)
    def _():
        yc_ref[...] = contrib

    @pl.when(jnp.logical_and(f > 0, f < last_f))
    def _():
        yc_ref[...] += contrib

    @pl.when(f == last_f)
    def _():
        total = (yc_ref[...] + contrib).astype(BF16)
        for hh in range(MOE_HALVES):
            o_ref[half_rows[hh], :] += _dot(place_of(hh, 0.0, MOE_MAIN), total[main_rows[hh], :])

    lane1 = lax.broadcasted_iota(jnp.int32, (1, LANES), 1)
    for hh in range(MOE_HALVES):
        n_routed = jnp.sum(jnp.where(lane1 == e, cnt_ref[hh, 0:1, :], 0.0)).astype(jnp.int32)
        n_over = jnp.maximum(n_routed - MOE_MAIN + (MOE_OVER - 1), 0) // MOE_OVER

        def over_body(b, carry, hh=hh):
            first = (MOE_MAIN + b * MOE_OVER).astype(F32)
            pick = pick_of(hh, first, MOE_OVER)
            xo = _dot(pick, xn_ref[half_rows[hh], :]).astype(BF16)
            co = expert(xo, _dot_sel_left(pick, gates_ref[hh])).astype(BF16)
            o_ref[half_rows[hh], :] += _dot(place_of(hh, first, MOE_OVER), co)
            return carry

        lax.fori_loop(0, n_over, over_body, 0)

    last_step = jnp.logical_and(e == pl.num_programs(1) - 1, f == last_f)
    if final_norm:
        @pl.when(last_step)
        def _():
            o_ref[...] = _rms(o_ref[...], fg_ref[...])

    if tail_rows:
        @pl.when(jnp.logical_and(last_step, pl.program_id(0) == pl.num_programs(0) - 1))
        def _():
            tail_ref[...] = o_ref[MOE_TILE - tail_rows:, :]


def _moe(x, g, router, w1, w3, w2, final_g, final_norm, head_rows=None):
    t = x.shape[0]
    tail_rows = 0 if head_rows is None else t - head_rows
    assert tail_rows == 0 or (tail_rows <= MOE_TILE and tail_rows % SUBLANES == 0)
    r = jnp.arange(MOE_HALF)
    before = (r[None, :] < r[:, None]).astype(BF16)
    once = pl.Buffered(1)
    out_specs = [pl.BlockSpec((MOE_TILE, D_MODEL), lambda i, e, f: (i, 0), pipeline_mode=once)]
    out_shape = [jax.ShapeDtypeStruct((t - tail_rows, D_MODEL), F32)]
    if tail_rows:
        out_specs.append(pl.BlockSpec((tail_rows, D_MODEL), lambda i, e, f: (0, 0), pipeline_mode=once))
        out_shape.append(jax.ShapeDtypeStruct((tail_rows, D_MODEL), F32))
    return pl.pallas_call(
        functools.partial(_moe_kernel, final_norm=final_norm, tail_rows=tail_rows),
        grid=(t // MOE_TILE, N_EXPERTS, D_FF // FF_BLOCK),
        in_specs=[
            pl.BlockSpec((MOE_TILE, D_MODEL), lambda i, e, f: (i, 0), pipeline_mode=once),
            pl.BlockSpec((1, D_MODEL), lambda i, e, f: (0, 0)),
            pl.BlockSpec((D_MODEL, LANES), lambda i, e, f: (0, 0)),
            pl.BlockSpec((MOE_HALF, MOE_HALF), lambda i, e, f: (0, 0), pipeline_mode=once),
            pl.BlockSpec((1, D_MODEL, FF_BLOCK), lambda i, e, f: (e, 0, f)),
            pl.BlockSpec((1, D_MODEL, FF_BLOCK), lambda i, e, f: (e, 0, f)),
            pl.BlockSpec((1, FF_BLOCK, D_MODEL), lambda i, e, f: (e, f, 0)),
            pl.BlockSpec((1, D_MODEL), lambda i, e, f: (0, 0)),
        ],
        out_specs=out_specs,
        out_shape=out_shape,
        scratch_shapes=[
            pltpu.VMEM((MOE_TILE, D_MODEL), BF16),
            pltpu.VMEM((MOE_HALVES, MOE_HALF, LANES), F32),
            pltpu.VMEM((MOE_HALVES, MOE_HALF, LANES), F32),
            pltpu.VMEM((MOE_HALVES, LANES, MOE_HALF), F32),
            pltpu.VMEM((MOE_HALVES, SUBLANES, LANES), F32),
            pltpu.VMEM((MOE_HALVES * MOE_MAIN, D_MODEL), BF16),
            pltpu.VMEM((MOE_HALVES * MOE_MAIN, LANES), F32),
            pltpu.VMEM((MOE_HALVES * MOE_MAIN, D_MODEL), F32),
        ],
        compiler_params=_cparams(("arbitrary", "arbitrary", "arbitrary")),
        name="moe",
    )(x, g, router, before, w1, w3, w2, final_g)


def _pack_w_in(w):
    main = jnp.concatenate([
        w[:, OFF_BQKV:OFF_BQKV + DN_QKV],
        w[:, OFF_C:OFF_C + 3 * ATT_W],
        w[:, OFF_GATE:OFF_GATE + N_BRANCH * D_MODEL],
        w[:, OFF_A:OFF_A + 2 * CONV_CH],
        w[:, OFF_BG:OFF_BG + DN_HEADS * DN_DV],
    ], axis=1).astype(BF16)
    wa = w[:, OFF_BA:OFF_BA + DN_HEADS]
    wb = w[:, OFF_BB:OFF_BB + DN_HEADS]
    zc = jnp.zeros((D_MODEL, LANES - DN_HEADS), w.dtype)
    wab = jnp.concatenate([wa, zc, wb, zc], axis=1).astype(BF16)
    zr = jnp.zeros((SUBLANES - DN_HEADS, D_MODEL), w.dtype)
    wabt = jnp.concatenate([wa.T, zr, wb.T, zr], axis=0).astype(BF16)
    return main, wab, wabt


def _router_parts(router):
    parts = jnp.concatenate(_split3(router), axis=1)
    return jnp.pad(parts, ((0, 0), (0, LANES - parts.shape[1])))


def _lane_row(v):
    return jnp.pad(v.astype(F32), (0, LANES - v.shape[0]))[None, :]


def _sublane_col(v):
    return jnp.pad(v.astype(F32), (0, SUBLANES - v.shape[0]))[:, None]


def kernel(x_prompt, x_sample, cache_conv_a, state_sconv_b, state_delta_b, cache_k_c, cache_v_c, norm1_g, w_in, b_gate, dw_a, dwb_a, ln_a_g, ln_a_b, w_a_out, dw_b, a_log, dt_bias, onorm_b, w_b_out, rel_bias, w_c_out, w_out, norm2_g, ffn_w1, ffn_w3, ffn_w2, router, moe_w1, moe_w3, moe_w2, final_norm_g):
    bp, seq_len, _ = x_prompt.shape
    n_samp, samp_len, _ = x_sample.shape
    depth = w_in.shape[0]
    assert bp == 1 and samp_len == CHUNK and seq_len % TILE == 0 and (n_samp * CHUNK) % TILE == 0
    assert (seq_len + n_samp * CHUNK) % MOE_TILE == 0
    n_prompt_chunks = seq_len // CHUNK
    n_prompt = seq_len
    keep = min(ATT_PREV * CHUNK, seq_len)
    dt = x_prompt.dtype

    x = (x_prompt.reshape(seq_len, D_MODEL), x_sample.reshape(n_samp * CHUNK, D_MODEL))
    n_first = seq_len // TILE
    fg = final_norm_g[None, :]
    states = []
    for l in range(depth):
        w_main, wab, wabt = _pack_w_in(w_in[l])
        h, ab, abt = _in_proj(x, norm1_g[l][None, :], w_main, wab, wabt, n_first)

        hist_a = jnp.pad(jnp.concatenate([jnp.zeros((1,) + cache_conv_a.shape[2:], dt), cache_conv_a[l]], axis=0),
                         ((0, 0), (A_HIST - (CONV_WIDTH - 1), 0), (0, 0)))
        act_a, st_a = _conv_a(h, hist_a, dw_a[l], dwb_a[l][None, :], ln_a_g[l][None, :], ln_a_b[l][None, :],
                              n_prompt_chunks)

        hist_b = jnp.pad(jnp.concatenate([jnp.zeros((1,) + state_sconv_b.shape[2:], dt), state_sconv_b[l]], axis=0),
                         ((0, 0), (B_HIST - (DN_CONV - 1), 0), (0, 0)))
        s0 = jnp.concatenate([jnp.zeros((1,) + state_delta_b.shape[2:], dt), state_delta_b[l]], axis=0)
        act_b, st_sc, st_s = _delta(h, ab, abt, hist_b, s0, dw_b[l], _lane_row(a_log[l]), _lane_row(dt_bias[l]),
                                    _sublane_col(a_log[l]), _sublane_col(dt_bias[l]), onorm_b[l][None, :],
                                    n_prompt_chunks)

        bias = _bias_table(rel_bias[l])
        ck = cache_k_c[l].reshape(n_samp, -1, ATT_W).astype(BF16)
        cv = cache_v_c[l].reshape(n_samp, -1, ATT_W).astype(BF16)
        act_c = (_attn_prompt(h, bias, n_prompt_chunks), _attn_sample(h, ck, cv, bias, n_prompt_chunks))

        xm = _merge(x, act_a, act_b, act_c, h, w_a_out[l].astype(BF16), w_b_out[l].astype(BF16),
                    w_c_out[l].astype(BF16), w_out[l].astype(BF16), b_gate[l], n_first)

        last = l == depth - 1
        j = l // 2
        if l % 2 == 0:
            x = (_ffn(xm, norm2_g[l][None, :], ffn_w1[j].astype(BF16), ffn_w3[j].astype(BF16),
                      ffn_w2[j].astype(BF16), fg, last),)
        else:
            x = tuple(_moe(xm, norm2_g[l][None, :], _router_parts(router[j]),
                           moe_w1[j].astype(BF16), moe_w3[j].astype(BF16), moe_w2[j].astype(BF16), fg, last,
                           head_rows=n_prompt if last else None))

        kv_new = h[n_prompt - keep:, H_C + ATT_W:H_C + 3 * ATT_W].astype(dt)
        states.append(dict(
            conv=st_a[:, A_HIST - (CONV_WIDTH - 1):, :],
            sconv=st_sc[:, B_HIST - (DN_CONV - 1):, :],
            delta=st_s,
            k=kv_new[:, :ATT_W], v=kv_new[:, ATT_W:]))

    def stack(fn):
        return jnp.stack([fn(s) for s in states])

    heads = (ATT_HEADS, ATT_HD)
    y_head, y_tail = x if len(x) == 2 else (x[0][:n_prompt], x[0][n_prompt:])
    y_prompt = y_head.reshape(1, seq_len, D_MODEL)
    y_sample = y_tail.reshape(n_samp, CHUNK, D_MODEL)
    return (
        y_prompt, y_sample,
        stack(lambda s: s["conv"][:1]), stack(lambda s: s["sconv"][:1]), stack(lambda s: s["delta"][:1]),
        stack(lambda s: s["k"][:keep].reshape(1, keep, *heads)),
        stack(lambda s: s["v"][:keep].reshape(1, keep, *heads)),
        stack(lambda s: s["conv"][1:]), stack(lambda s: s["sconv"][1:]), stack(lambda s: s["delta"][1:]),
        stack(lambda s: s["k"][keep:].reshape(n_samp, CHUNK, *heads)),
        stack(lambda s: s["v"][keep:].reshape(n_samp, CHUNK, *heads)),
    )
```

```python
import functools

import jax
import jax.numpy as jnp
from jax import lax
from jax.experimental import pallas as pl
from jax.experimental.pallas import tpu as pltpu

F32 = jnp.float32
BF16 = jnp.bfloat16

D_MODEL = 1024
CHUNK = 64
CONV_CH = 512
CONV_WIDTH = 31
DN_HEADS = 4
DN_DK = 128
DN_DV = 128
DN_CONV = 4
DN_QK = DN_HEADS * DN_DK
DN_QKV = DN_HEADS * (2 * DN_DK + DN_DV)
ATT_HEADS = 8
ATT_HD = 64
ATT_W = ATT_HEADS * ATT_HD
ATT_PREV = 8
BAND = (ATT_PREV + 1) * CHUNK
REL_MAX = 128
REL_MIN = -(CHUNK - 1)
N_REL = REL_MAX - REL_MIN + 1
D_FF = 2816
N_EXPERTS = 8
N_BRANCH = 3
EPS = 1e-6
NEG_INF = -1e30

OFF_A = 0
OFF_BQKV = OFF_A + 2 * CONV_CH
OFF_BA = OFF_BQKV + DN_QKV
OFF_BB = OFF_BA + DN_HEADS
OFF_BG = OFF_BB + DN_HEADS
OFF_C = OFF_BG + DN_HEADS * DN_DV
OFF_GATE = OFF_C + 3 * ATT_W

H_BQKV = 0
H_C = 1536
H_GATE = 3072
H_A = 6144
H_BG = 7168
H_COLS = 7680
H_BLOCK = 1536

LANES = 128
SUBLANES = 8
TILE = 512
CPT = TILE // CHUNK
FF_BLOCK = D_FF // 2
FF_CHUNK = 256
VMEM_LIMIT = 48 * 1024 * 1024


def _cparams(sem):
    return pltpu.CompilerParams(dimension_semantics=sem, vmem_limit_bytes=VMEM_LIMIT)


def _split3(x):
    hi = x.astype(BF16)
    r1 = x - hi.astype(F32)
    mid = r1.astype(BF16)
    lo = (r1 - mid.astype(F32)).astype(BF16)
    return hi, mid, lo


def _dot(a, b):
    return jnp.dot(a, b, preferred_element_type=F32)


def _dot_nt(a, b):
    return lax.dot_general(a, b, (((1,), (1,)), ((), ())), preferred_element_type=F32)


def _dot_tn(a, b):
    return lax.dot_general(a, b, (((0,), (0,)), ((), ())), preferred_element_type=F32)


def _dot_sel_left(sel_bf16, x):
    hi, mid, lo = _split3(x)
    return _dot(sel_bf16, hi) + _dot(sel_bf16, mid) + _dot(sel_bf16, lo)


def _dot_sel_right(x, sel_bf16):
    hi, mid, lo = _split3(x)
    return _dot(hi, sel_bf16) + _dot(mid, sel_bf16) + _dot(lo, sel_bf16)


def _sigmoid(x):
    return 1.0 / (1.0 + jnp.exp(-x))


def _silu(x):
    return x * _sigmoid(x)


def _stream_specs(parts, width, n_first):
    if len(parts) == 1:
        return [pl.BlockSpec((TILE, width), lambda i: (i, 0))]
    return [pl.BlockSpec((TILE, width), lambda i: (jnp.minimum(i, n_first - 1), 0)),
            pl.BlockSpec((TILE, width), lambda i: (jnp.maximum(i - n_first, 0), 0))]


def _stream_tile(refs, n_first):
    if len(refs) == 1:
        return refs[0][...]
    return jnp.where(pl.program_id(0) < n_first, refs[0][...], refs[1][...])


def _in_proj_kernel(*refs, n_x, n_w, n_first):
    x_refs, refs = refs[:n_x], refs[n_x:]
    g_ref, refs = refs[0], refs[1:]
    w_refs, (wab_ref, wabt_ref, h_ref, ab_ref, abt_ref) = refs[:n_w], refs[n_w:]
    x = _stream_tile(x_refs, n_first)
    xnb = (x * lax.rsqrt(jnp.mean(x * x, axis=-1, keepdims=True) + EPS) * g_ref[...]).astype(BF16)
    ab_ref[...] = _dot(xnb, wab_ref[...])
    abt_ref[...] = _dot_nt(wabt_ref[...], xnb)
    col0 = 0
    for w_ref in w_refs:
        width = w_ref.shape[1]
        for c in range(0, width, H_BLOCK):
            cols = slice(c, min(c + H_BLOCK, width))
            h_ref[:, col0 + cols.start:col0 + cols.stop] = _dot(xnb, w_ref[:, cols]).astype(BF16)
        col0 += width


def _in_proj(x_parts, g, w_sections, wab, wabt, n_first):
    t = sum(p.shape[0] for p in x_parts)
    assert sum(w.shape[1] for w in w_sections) == H_COLS
    once = pl.Buffered(1)
    return pl.pallas_call(
        functools.partial(_in_proj_kernel, n_x=len(x_parts), n_w=len(w_sections), n_first=n_first),
        grid=(t // TILE,),
        in_specs=[
            *_stream_specs(x_parts, D_MODEL, n_first),
            pl.BlockSpec((1, D_MODEL), lambda i: (0, 0)),
            *[pl.BlockSpec(w.shape, lambda i: (0, 0), pipeline_mode=once) for w in w_sections],
            pl.BlockSpec((D_MODEL, 2 * LANES), lambda i: (0, 0)),
            pl.BlockSpec((2 * SUBLANES, D_MODEL), lambda i: (0, 0)),
        ],
        out_specs=[
            pl.BlockSpec((TILE, H_COLS), lambda i: (i, 0)),
            pl.BlockSpec((TILE, 2 * LANES), lambda i: (i, 0)),
            pl.BlockSpec((2 * SUBLANES, TILE), lambda i: (0, i)),
        ],
        out_shape=[
            jax.ShapeDtypeStruct((t, H_COLS), BF16),
            jax.ShapeDtypeStruct((t, 2 * LANES), F32),
            jax.ShapeDtypeStruct((2 * SUBLANES, t), F32),
        ],
        compiler_params=_cparams(("parallel",)),
        name="in_proj",
    )(*x_parts, g, *w_sections, wab, wabt)


A_HIST = 32


def _chunk_flags(c, n_prompt_chunks):
    is_start = jnp.logical_or(c == 0, c >= n_prompt_chunks)
    is_end = c >= n_prompt_chunks - 1
    seq = jnp.maximum(c - (n_prompt_chunks - 1), 0)
    return is_start, is_end, seq


def _conv_a_kernel(h_ref, hist_ref, dw_ref, dwb_ref, lng_ref, lnb_ref, act_ref, st_ref,
                   xp_ref, cv_ref, *, n_prompt_chunks):
    i = pl.program_id(0)
    hv = h_ref[...]
    xp_ref[A_HIST:A_HIST + TILE, :] = (
        hv[:, :CONV_CH].astype(F32) * _sigmoid(hv[:, CONV_CH:].astype(F32)))
    first_tap = A_HIST - (CONV_WIDTH - 1)
    for j in range(CPT):
        base = CHUNK * j
        is_start, is_end, seq = _chunk_flags(CPT * i + j, n_prompt_chunks)

        @pl.when(is_start)
        def _():
            xp_ref[base:base + A_HIST, :] = hist_ref[seq]

        for grp in range(CONV_CH // LANES):
            cols = slice(grp * LANES, (grp + 1) * LANES)
            y = None
            for b in range(SUBLANES):
                n_rows = CHUNK if b == 0 else CHUNK + SUBLANES
                z = None
                for a in range((first_tap + CONV_WIDTH - 1) // SUBLANES + 1):
                    k = SUBLANES * a + b - first_tap
                    if 0 <= k < CONV_WIDTH:
                        row0 = base + SUBLANES * a
                        term = xp_ref[row0:row0 + n_rows, cols] * dw_ref[k:k + 1, cols]
                        z = term if z is None else z + term
                zb = z[b:b + CHUNK]
                y = zb if y is None else y + zb
            cv_ref[base:base + CHUNK, cols] = y

        @pl.when(is_end)
        def _():
            st_ref[seq] = xp_ref[base + CHUNK:base + CHUNK + A_HIST, :]

    xp_ref[0:A_HIST, :] = xp_ref[TILE:TILE + A_HIST, :]
    y = cv_ref[...] + dwb_ref[...]
    yc = y - jnp.mean(y, axis=-1, keepdims=True)
    yn = yc * lax.rsqrt(jnp.mean(yc * yc, axis=-1, keepdims=True) + EPS)
    act_ref[...] = _silu(yn * lng_ref[...] + lnb_ref[...]).astype(BF16)


def _conv_a(h, hist, dw, dwb, lng, lnb, n_prompt_chunks):
    t = h.shape[0]
    n_seq = hist.shape[0]
    full = lambda *shape: pl.BlockSpec(shape, lambda i: (0,) * len(shape))
    return pl.pallas_call(
        functools.partial(_conv_a_kernel, n_prompt_chunks=n_prompt_chunks),
        grid=(t // TILE,),
        in_specs=[
            pl.BlockSpec((TILE, 2 * CONV_CH), lambda i: (i, H_A // (2 * CONV_CH))),
            full(n_seq, A_HIST, CONV_CH),
            full(CONV_WIDTH, CONV_CH),
            full(1, CONV_CH), full(1, CONV_CH), full(1, CONV_CH),
        ],
        out_specs=[
            pl.BlockSpec((TILE, CONV_CH), lambda i: (i, 0)),
            full(n_seq, A_HIST, CONV_CH),
        ],
        out_shape=[
            jax.ShapeDtypeStruct((t, CONV_CH), BF16),
            jax.ShapeDtypeStruct((n_seq, A_HIST, CONV_CH), F32),
        ],
        scratch_shapes=[
            pltpu.VMEM((A_HIST + TILE, CONV_CH), F32),
            pltpu.VMEM((TILE, CONV_CH), F32),
        ],
        compiler_params=_cparams(("arbitrary",)),
        name="conv_a",
    )(h, hist, dw, dwb, lng, lnb)


B_HIST = SUBLANES


def _delta_kernel(qkv_ref, ab_ref, abt_ref, gate_ref, hist_ref, s0_ref, dw_ref,
                  alog_r_ref, dtb_r_ref, alog_c_ref, dtb_c_ref, onorm_ref, sel_ref,
                  ob_ref, sc_out_ref, s_out_ref,
                  xq_ref, pc_ref, qb_ref, kb_ref, kbeta_ref, vb_ref, kbe_ref, qg_ref, kd_ref,
                  gcx_ref, dl_ref, u_ref, w_ref, qk_ref, s_ref, *, n_prompt_chunks):
    i = pl.program_id(0)
    first_tap = B_HIST - (DN_CONV - 1)

    @pl.when(i == 0)
    def _():
        s_ref[...] = jnp.zeros(s_ref.shape, F32)

    xq_ref[B_HIST:B_HIST + TILE, :] = qkv_ref[...].astype(F32)
    for j in range(CPT):
        base = CHUNK * j
        is_start, is_end, seq = _chunk_flags(CPT * i + j, n_prompt_chunks)

        @pl.when(is_start)
        def _():
            xq_ref[base:base + B_HIST, :] = hist_ref[seq]

        for grp in range(DN_QKV // LANES):
            cols = slice(grp * LANES, (grp + 1) * LANES)
            acc = jnp.zeros((CHUNK, LANES), F32)
            for k in range(DN_CONV):
                off = base + first_tap + k
                acc = acc + xq_ref[off:off + CHUNK, cols] * dw_ref[k:k + 1, cols]
            pc_ref[base:base + CHUNK, cols] = acc

        @pl.when(is_end)
        def _():
            sc_out_ref[seq] = xq_ref[base + CHUNK:base + CHUNK + B_HIST, :]

    xq_ref[0:B_HIST, :] = xq_ref[TILE:TILE + B_HIST, :]

    for grp in range(DN_QKV // LANES):
        cols = slice(grp * LANES, (grp + 1) * LANES)
        x = _silu(pc_ref[:, cols])
        if grp < 2 * DN_HEADS:
            x = x * lax.rsqrt(jnp.sum(x * x, axis=-1, keepdims=True) + EPS)
        if grp < DN_HEADS:
            x = x * (DN_DK ** -0.5)
        pc_ref[:, cols] = x

    def log_decay(alpha, alog, dtb):
        z = alpha + dtb
        softplus = jnp.maximum(z, 0.0) + jnp.log(1.0 + jnp.exp(-jnp.abs(z)))
        return -jnp.exp(alog) * softplus

    ab = ab_ref[...]
    g_col = log_decay(ab[:, :LANES], alog_r_ref[...], dtb_r_ref[...])
    beta = _sigmoid(ab[:, LANES:])
    abt = abt_ref[...]
    g_row = log_decay(abt[:SUBLANES, :], alog_c_ref[...], dtb_c_ref[...])
    gcc = _dot_sel_left(sel_ref[0], g_col)
    gc_row = _dot_sel_right(g_row, sel_ref[1])
    gcl = _dot_sel_left(sel_ref[2], gcc)

    def expansion(width):
        hr = lax.broadcasted_iota(jnp.int32, (LANES, DN_HEADS * width), 0)
        hc = lax.broadcasted_iota(jnp.int32, (LANES, DN_HEADS * width), 1)
        return jnp.where(hc // width == hr, 1.0, 0.0).astype(BF16)

    wide = expansion(DN_DK)
    beta_x = _dot_sel_right(beta, wide)
    eg_x = _dot_sel_right(jnp.exp(gcc), wide)
    ekd_x = _dot_sel_right(jnp.exp(gcl - gcc), wide)
    dl_ref[...] = _dot_sel_right(jnp.exp(gcl), wide)
    gcx_ref[...] = _dot_sel_right(gcc, expansion(CHUNK))

    q = pc_ref[:, 0:DN_QK]
    k = pc_ref[:, DN_QK:2 * DN_QK]
    kbeta = k * beta_x
    qb_ref[...] = q.astype(BF16)
    kb_ref[...] = k.astype(BF16)
    kbeta_ref[...] = kbeta.astype(BF16)
    vb_ref[...] = (pc_ref[:, 2 * DN_QK:] * beta_x).astype(BF16)
    kbe_ref[...] = (kbeta * eg_x).astype(BF16)
    qg_ref[...] = (q * eg_x).astype(BF16)
    kd_ref[...] = (k * ekd_x).astype(BF16)

    def block_diag(y, width):
        blk = lax.broadcasted_iota(jnp.int32, y.shape, 1) // width
        zero = jnp.zeros((), y.dtype)
        return jnp.concatenate([jnp.where(blk == h, y, zero) for h in range(DN_HEADS)], axis=0)

    row4 = lax.broadcasted_iota(jnp.int32, (CHUNK, DN_HEADS * CHUNK), 0)
    col4 = lax.broadcasted_iota(jnp.int32, (CHUNK, DN_HEADS * CHUNK), 1) % CHUNK
    eye4 = jnp.where(row4 == col4, 1.0, 0.0).astype(F32)

    chunk_rows = [slice(CHUNK * j, CHUNK * (j + 1)) for j in range(CPT)]
    n_pow, t_inv = [None] * CPT, [None] * CPT
    for j, rows in enumerate(chunk_rows):
        lhs = jnp.concatenate([kbeta_ref[rows, :], qb_ref[rows, :]], axis=0)
        both = _dot_nt(lhs, block_diag(kb_ref[rows, :], DN_DK))
        gc_r = jnp.concatenate([gc_row[h:h + 1, rows] for h in range(DN_HEADS)], axis=1)
        gam = jnp.exp(jnp.where(row4 >= col4, gcx_ref[rows, :] - gc_r, -jnp.inf))
        qk_ref[j] = (both[CHUNK:] * gam).astype(BF16)
        n_pow[j] = jnp.where(row4 > col4, -(both[:CHUNK] * gam), 0.0)
        t_inv[j] = eye4 + n_pow[j]
    for j in range(CPT):
        nb = n_pow[j].astype(BF16)
        n_pow[j] = _dot(nb, block_diag(nb, CHUNK))
    for _ in range(4):
        for j in range(CPT):
            nb = n_pow[j].astype(BF16)
            res = _dot(jnp.concatenate([nb, t_inv[j].astype(BF16)], axis=0), block_diag(nb, CHUNK))
            n_pow[j] = res[:CHUNK]
            t_inv[j] = t_inv[j] + res[CHUNK:]
    for j, rows in enumerate(chunk_rows):
        t_fin = t_inv[j] + _dot(t_inv[j].astype(BF16), block_diag(n_pow[j].astype(BF16), CHUNK))
        tb = t_fin.astype(BF16)
        u_ref[rows, :] = _dot(tb, block_diag(vb_ref[rows, :], DN_DV))
        w_ref[rows, :] = _dot(tb, block_diag(kbe_ref[rows, :], DN_DK)).astype(BF16)

    for j, rows in enumerate(chunk_rows):
        is_start, _, seq = _chunk_flags(CPT * i + j, n_prompt_chunks)
        vns, outs = [], []
        for h in range(DN_HEADS):
            hc = slice(h * DN_DK, (h + 1) * DN_DK)
            s = jnp.where(is_start, s0_ref[seq, h], s_ref[h])
            res = _dot(jnp.concatenate([w_ref[rows, hc], qg_ref[rows, hc]], axis=0), s.astype(BF16))
            vnb = (u_ref[rows, hc] - res[:CHUNK]).astype(BF16)
            s_new = s * dl_ref[CHUNK * j:CHUNK * j + 1, hc] + _dot_tn(kd_ref[rows, hc], vnb)
            s_ref[h] = s_new
            s_out_ref[seq, h] = s_new
            vns.append(vnb)
            outs.append(res[CHUNK:])
        vn_all = jnp.concatenate(vns, axis=1)
        o_all = jnp.concatenate(outs, axis=1) + _dot(qk_ref[j], block_diag(vn_all, DN_DV))
        for h in range(DN_HEADS):
            hc = slice(h * DN_DV, (h + 1) * DN_DV)
            o = o_all[:, hc]
            on = o * lax.rsqrt(jnp.mean(o * o, axis=-1, keepdims=True) + EPS) * onorm_ref[...]
            ob_ref[rows, hc] = (on * _silu(gate_ref[rows, hc].astype(F32))).astype(BF16)


def _chunk_selectors():
    r = jnp.arange(TILE)[:, None]
    c = jnp.arange(TILE)[None, :]
    same_chunk = (r // CHUNK) == (c // CHUNK)
    low = same_chunk & (c <= r)
    upp = same_chunk & (r <= c)
    last = c == (r // CHUNK) * CHUNK + (CHUNK - 1)
    return jnp.stack([low, upp, last]).astype(BF16)


def _delta(h, ab, abt, hist, s0, dw, alog_r, dtb_r, alog_c, dtb_c, onorm, n_prompt_chunks):
    t = h.shape[0]
    n_seq = hist.shape[0]
    full = lambda *shape: pl.BlockSpec(shape, lambda i: (0,) * len(shape))
    return pl.pallas_call(
        functools.partial(_delta_kernel, n_prompt_chunks=n_prompt_chunks),
        grid=(t // TILE,),
        in_specs=[
            pl.BlockSpec((TILE, DN_QKV), lambda i: (i, H_BQKV // DN_QKV)),
            pl.BlockSpec((TILE, 2 * LANES), lambda i: (i, 0)),
            pl.BlockSpec((2 * SUBLANES, TILE), lambda i: (0, i)),
            pl.BlockSpec((TILE, DN_HEADS * DN_DV), lambda i: (i, H_BG // (DN_HEADS * DN_DV))),
            full(n_seq, B_HIST, DN_QKV),
            full(n_seq, DN_HEADS, DN_DK, DN_DV),
            full(DN_CONV, DN_QKV),
            full(1, LANES), full(1, LANES), full(SUBLANES, 1), full(SUBLANES, 1),
            full(1, DN_DV),
            full(3, TILE, TILE),
        ],
        out_specs=[
            pl.BlockSpec((TILE, DN_HEADS * DN_DV), lambda i: (i, 0)),
            full(n_seq, B_HIST, DN_QKV),
            full(n_seq, DN_HEADS, DN_DK, DN_DV),
        ],
        out_shape=[
            jax.ShapeDtypeStruct((t, DN_HEADS * DN_DV), BF16),
            jax.ShapeDtypeStruct((n_seq, B_HIST, DN_QKV), F32),
            jax.ShapeDtypeStruct((n_seq, DN_HEADS, DN_DK, DN_DV), F32),
        ],
        scratch_shapes=[
            pltpu.VMEM((B_HIST + TILE, DN_QKV), F32),
            pltpu.VMEM((TILE, DN_QKV), F32),
            *[pltpu.VMEM((TILE, DN_QK), BF16) for _ in range(7)],
            pltpu.VMEM((TILE, DN_HEADS * CHUNK), F32),
            pltpu.VMEM((TILE, DN_QK), F32),
            pltpu.VMEM((TILE, DN_HEADS * DN_DV), F32),
            pltpu.VMEM((TILE, DN_QK), BF16),
            pltpu.VMEM((CPT, CHUNK, DN_HEADS * CHUNK), BF16),
            pltpu.VMEM((DN_HEADS, DN_DK, DN_DV), F32),
        ],
        compiler_params=_cparams(("arbitrary",)),
        name="delta",
    )(h, ab, abt, h, hist, s0, dw, alog_r, dtb_r, alog_c, dtb_c, onorm, _chunk_selectors())


PAIR = 2 * CHUNK
PAIR_BAND = (ATT_PREV + 2) * CHUNK


def _bias_table_kernel(rb_ref, out_ref):
    rel = lax.broadcasted_iota(jnp.int32, (2 * LANES, PAIR_BAND), 0)
    key = lax.broadcasted_iota(jnp.int32, (2 * LANES, PAIR_BAND), 1)
    key_chunk = lax.broadcasted_iota(jnp.int32, (ATT_HEADS, PAIR_BAND), 1) // CHUNK
    rb = rb_ref[...]
    for a in range(PAIR):
        idx = jnp.clip(ATT_PREV * CHUNK + a - key, REL_MIN, REL_MAX) - REL_MIN
        onehot = jnp.where(rel == idx, 1.0, 0.0).astype(BF16)
        band_pos = key_chunk - a // CHUNK
        seen = jnp.logical_and(band_pos >= 0, band_pos <= ATT_PREV)
        out_ref[:, a, :] = jnp.where(seen, _dot_sel_right(rb, onehot), NEG_INF)


def _bias_table(rel_bias):
    rb = jnp.pad(rel_bias, ((0, 0), (0, 2 * LANES - N_REL)))
    return pl.pallas_call(
        _bias_table_kernel,
        out_shape=jax.ShapeDtypeStruct((ATT_HEADS, PAIR, PAIR_BAND), F32),
        compiler_params=pltpu.CompilerParams(vmem_limit_bytes=VMEM_LIMIT),
        name="bias_table",
    )(rb)


def _attend(q, kband_ref, vband_ref, row0, n_keys, bias_of_head, first_valid_col):
    n_q = q.shape[0]
    lane = lax.broadcasted_iota(jnp.int32, (n_q, LANES), 1)
    valid = lax.broadcasted_iota(jnp.int32, (n_q, n_keys), 1) >= first_valid_col
    zero = jnp.zeros((), BF16)
    qs = q * jnp.asarray(ATT_HD ** -0.5, BF16)
    heads = range(ATT_HEADS)
    col_of = lambda h: slice((h // 2) * LANES, (h // 2 + 1) * LANES)
    s = []
    for h in heads:
        in_head = (lane < ATT_HD) if h % 2 == 0 else (lane >= ATT_HD)
        k2 = kband_ref[pl.ds(row0, n_keys), col_of(h)]
        s.append(_dot_nt(jnp.where(in_head, qs[:, col_of(h)], zero), k2))
    s = [jnp.where(valid, s[h] + bias_of_head(h), NEG_INF) for h in heads]
    top = [jnp.max(s[h], axis=-1, keepdims=True) for h in heads]
    p = [jnp.exp(s[h] - top[h]) for h in heads]
    denom = [jnp.sum(p[h], axis=-1, keepdims=True) for h in heads]
    o = [_dot(p[h].astype(BF16), vband_ref[pl.ds(row0, n_keys), col_of(h)]) for h in heads]
    o = [o[h] / denom[h] for h in heads]
    return [jnp.where(lane < ATT_HD, o[2 * hp], o[2 * hp + 1]) for hp in range(ATT_HEADS // 2)]


def _attn_prompt_kernel(q_ref, kp_ref, kc_ref, vp_ref, vc_ref, bias_ref, o_ref, kb_ref, vb_ref):
    i = pl.program_id(0)
    kb_ref[0:TILE, :] = kp_ref[...]
    kb_ref[TILE:2 * TILE, :] = kc_ref[...]
    vb_ref[0:TILE, :] = vp_ref[...]
    vb_ref[TILE:2 * TILE, :] = vc_ref[...]

    def pair_body(p, carry):
        base = pl.multiple_of(p * PAIR, PAIR)
        rows = pl.ds(base, PAIR)
        first_valid = jnp.maximum(ATT_PREV - (CPT * i + 2 * p), 0) * CHUNK
        outs = _attend(q_ref[rows, :], kb_ref, vb_ref, base, PAIR_BAND, lambda h: bias_ref[h], first_valid)
        for hp, o in enumerate(outs):
            o_ref[rows, hp * LANES:(hp + 1) * LANES] = o.astype(BF16)
        return carry

    lax.fori_loop(0, TILE // PAIR, pair_body, 0)


def _attn_prompt(h, bias, n_prompt_chunks):
    n_tiles = n_prompt_chunks // CPT
    qb, kb, vb = (H_C // ATT_W, H_C // ATT_W + 1, H_C // ATT_W + 2)
    prev = lambda i: jnp.maximum(i - 1, 0)
    return pl.pallas_call(
        _attn_prompt_kernel,
        grid=(n_tiles,),
        in_specs=[
            pl.BlockSpec((TILE, ATT_W), lambda i: (i, qb)),
            pl.BlockSpec((TILE, ATT_W), lambda i: (prev(i), kb)),
            pl.BlockSpec((TILE, ATT_W), lambda i: (i, kb)),
            pl.BlockSpec((TILE, ATT_W), lambda i: (prev(i), vb)),
            pl.BlockSpec((TILE, ATT_W), lambda i: (i, vb)),
            pl.BlockSpec((ATT_HEADS, PAIR, PAIR_BAND), lambda i: (0, 0, 0)),
        ],
        out_specs=pl.BlockSpec((TILE, ATT_W), lambda i: (i, 0)),
        out_shape=jax.ShapeDtypeStruct((n_tiles * TILE, ATT_W), BF16),
        scratch_shapes=[pltpu.VMEM((2 * TILE, ATT_W), BF16), pltpu.VMEM((2 * TILE, ATT_W), BF16)],
        compiler_params=_cparams(("arbitrary",)),
        name="attn_prompt",
    )(h, h, h, h, h, bias)


def _attn_sample_kernel(q_ref, k_ref, v_ref, ck_ref, cv_ref, bias_ref, o_ref, kb_ref, vb_ref):
    kb_ref[0:ATT_PREV * CHUNK, :] = ck_ref[0]
    kb_ref[ATT_PREV * CHUNK:BAND, :] = k_ref[...]
    vb_ref[0:ATT_PREV * CHUNK, :] = cv_ref[0]
    vb_ref[ATT_PREV * CHUNK:BAND, :] = v_ref[...]
    outs = _attend(q_ref[...], kb_ref, vb_ref, 0, BAND, lambda h: bias_ref[h, 0:CHUNK, 0:BAND], 0)
    for hp, o in enumerate(outs):
        o_ref[:, hp * LANES:(hp + 1) * LANES] = o.astype(BF16)


def _attn_sample(h, cache_k, cache_v, bias, n_prompt_chunks):
    n_seq = cache_k.shape[0]
    qb, kb, vb = (H_C // ATT_W, H_C // ATT_W + 1, H_C // ATT_W + 2)
    return pl.pallas_call(
        _attn_sample_kernel,
        grid=(n_seq,),
        in_specs=[
            pl.BlockSpec((CHUNK, ATT_W), lambda b: (n_prompt_chunks + b, qb)),
            pl.BlockSpec((CHUNK, ATT_W), lambda b: (n_prompt_chunks + b, kb)),
            pl.BlockSpec((CHUNK, ATT_W), lambda b: (n_prompt_chunks + b, vb)),
            pl.BlockSpec((1, ATT_PREV * CHUNK, ATT_W), lambda b: (b, 0, 0)),
            pl.BlockSpec((1, ATT_PREV * CHUNK, ATT_W), lambda b: (b, 0, 0)),
            pl.BlockSpec((ATT_HEADS, PAIR, PAIR_BAND), lambda b: (0, 0, 0)),
        ],
        out_specs=pl.BlockSpec((CHUNK, ATT_W), lambda b: (b, 0)),
        out_shape=jax.ShapeDtypeStruct((n_seq * CHUNK, ATT_W), BF16),
        scratch_shapes=[pltpu.VMEM((BAND, ATT_W), BF16), pltpu.VMEM((BAND, ATT_W), BF16)],
        compiler_params=_cparams(("arbitrary",)),
        name="attn_sample",
    )(h, h, h, cache_k, cache_v, bias)


def _merge_kernel(*refs, n_x, n_c, n_first):
    x_refs, refs = refs[:n_x], refs[n_x:]
    a_ref, b_ref = refs[:2]
    c_refs, refs = refs[2:2 + n_c], refs[2 + n_c:]
    gate_ref, wa_ref, wb_ref, wc_ref, wo_ref, bg_ref, o_ref = refs
    acts = (a_ref[...], b_ref[...], _stream_tile(c_refs, n_first))
    merged = jnp.zeros((TILE, D_MODEL), F32)
    for n, (act, w_ref) in enumerate(zip(acts, (wa_ref, wb_ref, wc_ref))):
        y = _dot(act, w_ref[...])
        z = gate_ref[:, n * D_MODEL:(n + 1) * D_MODEL].astype(F32) + bg_ref[n:n + 1, :]
        merged = merged + _sigmoid(z) * y
    o_ref[...] = _stream_tile(x_refs, n_first) + _dot(merged.astype(BF16), wo_ref[...])


def _merge(x_parts, act_a, act_b, act_c_parts, h, wa, wb, wc, wo, b_gate, n_first):
    t = h.shape[0]
    full = lambda *shape: pl.BlockSpec(shape, lambda i: (0,) * len(shape))
    act = pl.BlockSpec((TILE, CONV_CH), lambda i: (i, 0))
    return pl.pallas_call(
        functools.partial(_merge_kernel, n_x=len(x_parts), n_c=len(act_c_parts), n_first=n_first),
        grid=(t // TILE,),
        in_specs=[
            *_stream_specs(x_parts, D_MODEL, n_first),
            act, act,
            *_stream_specs(act_c_parts, ATT_W, n_first),
            pl.BlockSpec((TILE, N_BRANCH * D_MODEL), lambda i: (i, H_GATE // (N_BRANCH * D_MODEL))),
            full(CONV_CH, D_MODEL), full(DN_HEADS * DN_DV, D_MODEL), full(ATT_W, D_MODEL),
            full(D_MODEL, D_MODEL), full(N_BRANCH, D_MODEL),
        ],
        out_specs=pl.BlockSpec((TILE, D_MODEL), lambda i: (i, 0)),
        out_shape=jax.ShapeDtypeStruct((t, D_MODEL), F32),
        compiler_params=_cparams(("parallel",)),
        name="merge",
    )(*x_parts, act_a, act_b, *act_c_parts, h, wa, wb, wc, wo, b_gate)


def _rms(x, g):
    return x * lax.rsqrt(jnp.mean(x * x, axis=-1, keepdims=True) + EPS) * g


def _ffn_kernel(x_ref, g_ref, w1_ref, w3_ref, w2_ref, fg_ref, o_ref, *, final_norm):
    x = x_ref[...]
    xn = _rms(x, g_ref[...]).astype(BF16)
    y = x
    for f in range(D_FF // FF_CHUNK):
        cols = slice(f * FF_CHUNK, (f + 1) * FF_CHUNK)
        hidden = _silu(_dot(xn, w1_ref[:, cols])) * _dot(xn, w3_ref[:, cols])
        y = y + _dot(hidden.astype(BF16), w2_ref[cols, :])
    o_ref[...] = _rms(y, fg_ref[...]) if final_norm else y


def _ffn(x, g, w1, w3, w2, final_g, final_norm):
    t = x.shape[0]
    once = pl.Buffered(1)
    return pl.pallas_call(
        functools.partial(_ffn_kernel, final_norm=final_norm),
        grid=(t // TILE,),
        in_specs=[
            pl.BlockSpec((TILE, D_MODEL), lambda i: (i, 0)),
            pl.BlockSpec((1, D_MODEL), lambda i: (0, 0)),
            pl.BlockSpec((D_MODEL, D_FF), lambda i: (0, 0), pipeline_mode=once),
            pl.BlockSpec((D_MODEL, D_FF), lambda i: (0, 0), pipeline_mode=once),
            pl.BlockSpec((D_FF, D_MODEL), lambda i: (0, 0), pipeline_mode=once),
            pl.BlockSpec((1, D_MODEL), lambda i: (0, 0)),
        ],
        out_specs=pl.BlockSpec((TILE, D_MODEL), lambda i: (i, 0)),
        out_shape=jax.ShapeDtypeStruct((t, D_MODEL), F32),
        compiler_params=_cparams(("parallel",)),
        name="ffn",
    )(x, g, w1, w3, w2, final_g)


MOE_HALF = 768
MOE_HALVES = 2
MOE_TILE = MOE_HALF * MOE_HALVES
MOE_MAIN = 224
MOE_OVER = 128


def _moe_kernel(x_ref, g_ref, r_ref, before_ref, w1_ref, w3_ref, w2_ref, fg_ref, o_ref, *rest,
                final_norm, tail_rows):
    if tail_rows:
        tail_ref, *rest = rest
    xn_ref, gates_ref, key_ref, keyt_ref, cnt_ref, xc_ref, gs_ref, yc_ref = rest
    e = pl.program_id(1)
    f = pl.program_id(2)
    last_f = pl.num_programs(2) - 1
    lane = lax.broadcasted_iota(jnp.int32, (MOE_HALF, LANES), 1)
    half_rows = [slice(hh * MOE_HALF, (hh + 1) * MOE_HALF) for hh in range(MOE_HALVES)]
    main_rows = [slice(hh * MOE_MAIN, (hh + 1) * MOE_MAIN) for hh in range(MOE_HALVES)]

    @pl.when(jnp.logical_and(e == 0, f == 0))
    def _():
        o_ref[...] = x_ref[...]
        r3 = r_ref[...]
        for hh, rows in enumerate(half_rows):
            xn = _rms(x_ref[rows, :], g_ref[...])
            xn_ref[rows, :] = xn.astype(BF16)
            xh, xm, xl = _split3(xn)
            ph, pm, pl_ = _dot(xh, r3), _dot(xm, r3), _dot(xl, r3)
            down = lambda p, k: pltpu.roll(p, LANES - k * N_EXPERTS, axis=1)
            logits = (ph + (down(ph, 1) + pm)) + (down(ph, 2) + down(pm, 1) + pl_)
            logits = jnp.where(lane < N_EXPERTS, logits, -jnp.inf)
            m1 = jnp.max(logits, axis=-1, keepdims=True)
            i1 = jnp.min(jnp.where(logits == m1, lane, LANES), axis=-1, keepdims=True)
            others = jnp.where(lane == i1, -jnp.inf, logits)
            m2 = jnp.max(others, axis=-1, keepdims=True)
            i2 = jnp.min(jnp.where(others == m2, lane, LANES), axis=-1, keepdims=True)
            e2 = jnp.exp(m2 - m1)
            w_top = 1.0 / (1.0 + e2)
            gates_ref[hh] = jnp.where(lane == i1, w_top, 0.0) + jnp.where(lane == i2, e2 * w_top, 0.0)
            routed = jnp.where(lane == i1, 1.0, jnp.where(lane == i2, 1.0, 0.0))
            key = jnp.where(routed > 0.0, _dot(before_ref[...], routed.astype(BF16)), -1.0)
            key_ref[hh] = key
            keyt_ref[hh] = key.T
            cnt_ref[hh] = jnp.broadcast_to(jnp.sum(routed, axis=0, keepdims=True), (SUBLANES, LANES))

    def pick_of(hh, first, n_rows):
        slot = lax.broadcasted_iota(jnp.int32, (n_rows, MOE_HALF), 0).astype(F32)
        return jnp.where(keyt_ref[hh, pl.ds(e, 1), :] - first == slot, 1.0, 0.0).astype(BF16)

    def place_of(hh, first, n_rows):
        key_col = jnp.sum(jnp.where(lane == e, key_ref[hh], 0.0), axis=-1, keepdims=True)
        slot = lax.broadcasted_iota(jnp.int32, (MOE_HALF, n_rows), 1).astype(F32)
        return jnp.where(key_col - first == slot, 1.0, 0.0).astype(BF16)

    def expert(xc, gs):
        lane_r = lax.broadcasted_iota(jnp.int32, gs.shape, 1)
        ge = jnp.sum(jnp.where(lane_r == e, gs, 0.0), axis=-1, keepdims=True)
        hidden = _silu(_dot(xc, w1_ref[0])) * _dot(xc, w3_ref[0])
        return _dot((hidden * ge).astype(BF16), w2_ref[0])

    @pl.when(f == 0)
    def _():
        for hh in range(MOE_HALVES):
            pick = pick_of(hh, 0.0, MOE_MAIN)
            xc_ref[main_rows[hh], :] = _dot(pick, xn_ref[half_rows[hh], :]).astype(BF16)
            gs_ref[main_rows[hh], :] = _dot_sel_left(pick, gates_ref[hh])

    contrib = expert(xc_ref[...], gs_ref[...])

    assert D_FF // FF_BLOCK >= 2

    @pl.when(f == 0)
    def _():
        yc_ref[...] = contrib

    @pl.when(jnp.logical_and(f > 0, f < last_f))
    def _():
        yc_ref[...] += contrib

    @pl.when(f == last_f)
    def _():
        total = (yc_ref[...] + contrib).astype(BF16)
        for hh in range(MOE_HALVES):
            o_ref[half_rows[hh], :] += _dot(place_of(hh, 0.0, MOE_MAIN), total[main_rows[hh], :])

    lane1 = lax.broadcasted_iota(jnp.int32, (1, LANES), 1)
    for hh in range(MOE_HALVES):
        n_routed = jnp.sum(jnp.where(lane1 == e, cnt_ref[hh, 0:1, :], 0.0)).astype(jnp.int32)
        n_over = jnp.maximum(n_routed - MOE_MAIN + (MOE_OVER - 1), 0) // MOE_OVER

        def over_body(b, carry, hh=hh):
            first = (MOE_MAIN + b * MOE_OVER).astype(F32)
            pick = pick_of(hh, first, MOE_OVER)
            xo = _dot(pick, xn_ref[half_rows[hh], :]).astype(BF16)
            co = expert(xo, _dot_sel_left(pick, gates_ref[hh])).astype(BF16)
            o_ref[half_rows[hh], :] += _dot(place_of(hh, first, MOE_OVER), co)
            return carry

        lax.fori_loop(0, n_over, over_body, 0)

    last_step = jnp.logical_and(e == pl.num_programs(1) - 1, f == last_f)
    if final_norm:
        @pl.when(last_step)
        def _():
            o_ref[...] = _rms(o_ref[...], fg_ref[...])

    if tail_rows:
        @pl.when(jnp.logical_and(last_step, pl.program_id(0) == pl.num_programs(0) - 1))
        def _():
            tail_ref[...] = o_ref[MOE_TILE - tail_rows:, :]


def _moe(x, g, router, w1, w3, w2, final_g, final_norm, head_rows=None):
    t = x.shape[0]
    tail_rows = 0 if head_rows is None else t - head_rows
    assert tail_rows == 0 or (tail_rows <= MOE_TILE and tail_rows % SUBLANES == 0)
    r = jnp.arange(MOE_HALF)
    before = (r[None, :] < r[:, None]).astype(BF16)
    once = pl.Buffered(1)
    out_specs = [pl.BlockSpec((MOE_TILE, D_MODEL), lambda i, e, f: (i, 0), pipeline_mode=once)]
    out_shape = [jax.ShapeDtypeStruct((t - tail_rows, D_MODEL), F32)]
    if tail_rows:
        out_specs.append(pl.BlockSpec((tail_rows, D_MODEL), lambda i, e, f: (0, 0), pipeline_mode=once))
        out_shape.append(jax.ShapeDtypeStruct((tail_rows, D_MODEL), F32))
    return pl.pallas_call(
        functools.partial(_moe_kernel, final_norm=final_norm, tail_rows=tail_rows),
        grid=(t // MOE_TILE, N_EXPERTS, D_FF // FF_BLOCK),
        in_specs=[
            pl.BlockSpec((MOE_TILE, D_MODEL), lambda i, e, f: (i, 0), pipeline_mode=once),
            pl.BlockSpec((1, D_MODEL), lambda i, e, f: (0, 0)),
            pl.BlockSpec((D_MODEL, LANES), lambda i, e, f: (0, 0)),
            pl.BlockSpec((MOE_HALF, MOE_HALF), lambda i, e, f: (0, 0), pipeline_mode=once),
            pl.BlockSpec((1, D_MODEL, FF_BLOCK), lambda i, e, f: (e, 0, f)),
            pl.BlockSpec((1, D_MODEL, FF_BLOCK), lambda i, e, f: (e, 0, f)),
            pl.BlockSpec((1, FF_BLOCK, D_MODEL), lambda i, e, f: (e, f, 0)),
            pl.BlockSpec((1, D_MODEL), lambda i, e, f: (0, 0)),
        ],
        out_specs=out_specs,
        out_shape=out_shape,
        scratch_shapes=[
            pltpu.VMEM((MOE_TILE, D_MODEL), BF16),
            pltpu.VMEM((MOE_HALVES, MOE_HALF, LANES), F32),
            pltpu.VMEM((MOE_HALVES, MOE_HALF, LANES), F32),
            pltpu.VMEM((MOE_HALVES, LANES, MOE_HALF), F32),
            pltpu.VMEM((MOE_HALVES, SUBLANES, LANES), F32),
            pltpu.VMEM((MOE_HALVES * MOE_MAIN, D_MODEL), BF16),
            pltpu.VMEM((MOE_HALVES * MOE_MAIN, LANES), F32),
            pltpu.VMEM((MOE_HALVES * MOE_MAIN, D_MODEL), F32),
        ],
        compiler_params=_cparams(("arbitrary", "arbitrary", "arbitrary")),
        name="moe",
    )(x, g, router, before, w1, w3, w2, final_g)


def _pack_w_in(w):
    sections = ((OFF_BQKV, DN_QKV), (OFF_C, 3 * ATT_W), (OFF_GATE, N_BRANCH * D_MODEL), (OFF_A, 2 * CONV_CH),
                (OFF_BG, DN_HEADS * DN_DV))

    def pack_kernel(w_ref, o_ref):
        col0 = 0
        for off, width in sections:
            o_ref[:, col0:col0 + width] = w_ref[:, off:off + width].astype(BF16)
            col0 += width

    rows = LANES
    main = (pl.pallas_call(
        pack_kernel,
        grid=(D_MODEL // rows,),
        in_specs=[pl.BlockSpec((rows, w.shape[1]), lambda i: (i, 0))],
        out_specs=pl.BlockSpec((rows, H_COLS), lambda i: (i, 0)),
        out_shape=jax.ShapeDtypeStruct((D_MODEL, H_COLS), BF16),
        compiler_params=_cparams(("parallel",)),
        name="pack_w_in",
    )(w),)
    wa = w[:, OFF_BA:OFF_BA + DN_HEADS]
    wb = w[:, OFF_BB:OFF_BB + DN_HEADS]
    zc = jnp.zeros((D_MODEL, LANES - DN_HEADS), w.dtype)
    wab = jnp.concatenate([wa, zc, wb, zc], axis=1).astype(BF16)
    zr = jnp.zeros((SUBLANES - DN_HEADS, D_MODEL), w.dtype)
    wabt = jnp.concatenate([wa.T, zr, wb.T, zr], axis=0).astype(BF16)
    return main, wab, wabt


def _router_parts(router):
    parts = jnp.concatenate(_split3(router), axis=1)
    return jnp.pad(parts, ((0, 0), (0, LANES - parts.shape[1])))


def _lane_row(v):
    return jnp.pad(v.astype(F32), (0, LANES - v.shape[0]))[None, :]


def _sublane_col(v):
    return jnp.pad(v.astype(F32), (0, SUBLANES - v.shape[0]))[:, None]


def kernel(x_prompt, x_sample, cache_conv_a, state_sconv_b, state_delta_b, cache_k_c, cache_v_c, norm1_g, w_in, b_gate, dw_a, dwb_a, ln_a_g, ln_a_b, w_a_out, dw_b, a_log, dt_bias, onorm_b, w_b_out, rel_bias, w_c_out, w_out, norm2_g, ffn_w1, ffn_w3, ffn_w2, router, moe_w1, moe_w3, moe_w2, final_norm_g):
    bp, seq_len, _ = x_prompt.shape
    n_samp, samp_len, _ = x_sample.shape
    depth = w_in.shape[0]
    assert bp == 1 and samp_len == CHUNK and seq_len % TILE == 0 and (n_samp * CHUNK) % TILE == 0
    assert (seq_len + n_samp * CHUNK) % MOE_TILE == 0
    n_prompt_chunks = seq_len // CHUNK
    n_prompt = seq_len
    keep = min(ATT_PREV * CHUNK, seq_len)
    dt = x_prompt.dtype

    x = (x_prompt.reshape(seq_len, D_MODEL), x_sample.reshape(n_samp * CHUNK, D_MODEL))
    n_first = seq_len // TILE
    fg = final_norm_g[None, :]
    states = []
    for l in range(depth):
        w_main, wab, wabt = _pack_w_in(w_in[l])
        h, ab, abt = _in_proj(x, norm1_g[l][None, :], w_main, wab, wabt, n_first)

        hist_a = jnp.pad(jnp.concatenate([jnp.zeros((1,) + cache_conv_a.shape[2:], dt), cache_conv_a[l]], axis=0),
                         ((0, 0), (A_HIST - (CONV_WIDTH - 1), 0), (0, 0)))
        act_a, st_a = _conv_a(h, hist_a, dw_a[l], dwb_a[l][None, :], ln_a_g[l][None, :], ln_a_b[l][None, :],
                              n_prompt_chunks)

        hist_b = jnp.pad(jnp.concatenate([jnp.zeros((1,) + state_sconv_b.shape[2:], dt), state_sconv_b[l]], axis=0),
                         ((0, 0), (B_HIST - (DN_CONV - 1), 0), (0, 0)))
        s0 = jnp.concatenate([jnp.zeros((1,) + state_delta_b.shape[2:], dt), state_delta_b[l]], axis=0)
        act_b, st_sc, st_s = _delta(h, ab, abt, hist_b, s0, dw_b[l], _lane_row(a_log[l]), _lane_row(dt_bias[l]),
                                    _sublane_col(a_log[l]), _sublane_col(dt_bias[l]), onorm_b[l][None, :],
                                    n_prompt_chunks)

        bias = _bias_table(rel_bias[l])
        ck = cache_k_c[l].reshape(n_samp, -1, ATT_W).astype(BF16)
        cv = cache_v_c[l].reshape(n_samp, -1, ATT_W).astype(BF16)
        act_c = (_attn_prompt(h, bias, n_prompt_chunks), _attn_sample(h, ck, cv, bias, n_prompt_chunks))

        xm = _merge(x, act_a, act_b, act_c, h, w_a_out[l].astype(BF16), w_b_out[l].astype(BF16),
                    w_c_out[l].astype(BF16), w_out[l].astype(BF16), b_gate[l], n_first)

        last = l == depth - 1
        j = l // 2
        if l % 2 == 0:
            x = (_ffn(xm, norm2_g[l][None, :], ffn_w1[j].astype(BF16), ffn_w3[j].astype(BF16),
                      ffn_w2[j].astype(BF16), fg, last),)
        else:
            x = tuple(_moe(xm, norm2_g[l][None, :], _router_parts(router[j]),
                           moe_w1[j].astype(BF16), moe_w3[j].astype(BF16), moe_w2[j].astype(BF16), fg, last,
                           head_rows=n_prompt if last else None))

        kv_new = h[n_prompt - keep:, H_C + ATT_W:H_C + 3 * ATT_W].astype(dt)
        states.append(dict(
            conv=st_a[:, A_HIST - (CONV_WIDTH - 1):, :],
            sconv=st_sc[:, B_HIST - (DN_CONV - 1):, :],
            delta=st_s,
            k=kv_new[:, :ATT_W], v=kv_new[:, ATT_W:]))

    def stack(fn):
        return jnp.stack([fn(s) for s in states])

    heads = (ATT_HEADS, ATT_HD)
    y_head, y_tail = x if len(x) == 2 else (x[0][:n_prompt], x[0][n_prompt:])
    y_prompt = y_head.reshape(1, seq_len, D_MODEL)
    y_sample = y_tail.reshape(n_samp, CHUNK, D_MODEL)
    return (
        y_prompt, y_sample,
        stack(lambda s: s["conv"][:1]), stack(lambda s: s["sconv"][:1]), stack(lambda s: s["delta"][:1]),
        stack(lambda s: s["k"][:keep].reshape(1, keep, *heads)),
        stack(lambda s: s["v"][:keep].reshape(1, keep, *heads)),
        stack(lambda s: s["conv"][1:]), stack(lambda s: s["sconv"][1:]), stack(lambda s: s["delta"][1:]),
        stack(lambda s: s["k"][keep:].reshape(n_samp, CHUNK, *heads)),
        stack(lambda s: s["v"][keep:].reshape(n_samp, CHUNK, *heads)),
    )
```
